```python
import jax, jax.numpy as jnp
from jax import lax
import numpy as np

D_MODEL = 2048
BATCH = 4
SEQ = 4096
DEPTH = 1

MLSTM_HEADS = 8
MLSTM_QK_DIM = 128
MLSTM_V_DIM = 256
MLSTM_QK_WIDTH = MLSTM_HEADS * MLSTM_QK_DIM
MLSTM_V_WIDTH = MLSTM_HEADS * MLSTM_V_DIM
CHUNK = 64
CONV_WIDTH = D_MODEL
CONV_GROUPS = 16
CONV_K = 3
EPS = 1e-6

SPLIT_WIDTHS = (
    MLSTM_QK_WIDTH, MLSTM_QK_WIDTH,
    MLSTM_V_WIDTH, MLSTM_V_WIDTH, MLSTM_V_WIDTH,
    MLSTM_HEADS, MLSTM_HEADS,
    CONV_WIDTH, CONV_WIDTH, CONV_WIDTH, CONV_WIDTH,
    D_MODEL, D_MODEL,
)
IN_WIDTH = sum(SPLIT_WIDTHS)

kernel_name = "hybrid_mlstm_shortconv_gated_block"


def _split_offsets():
    offs, acc = [], 0
    for w in SPLIT_WIDTHS:
        offs.append(acc)
        acc += w
    return offs


def rms_norm(x, w):
    xf = x.astype(jnp.float32)
    y = xf * lax.rsqrt(jnp.mean(xf * xf, axis=-1, keepdims=True) + EPS)
    return (y * w.astype(jnp.float32)).astype(x.dtype)


def mlstm_chunkwise(q, k, v, i_pre, f_pre):
    f32 = jnp.float32
    bsz, s, h, dk = q.shape
    dv = v.shape[-1]
    nc = s // CHUNK

    def to_chunks(t):
        t = t.astype(f32).reshape((bsz, nc, CHUNK, h) + t.shape[3:])
        return jnp.moveaxis(t, (1, 3), (0, 2))

    qc = to_chunks(q) * (dk ** -0.5)
    kc = to_chunks(k)
    vc = to_chunks(v)
    log_i = to_chunks(i_pre)
    log_f = to_chunks(jax.nn.log_sigmoid(f_pre.astype(f32)))
    causal = jnp.tril(jnp.ones((CHUNK, CHUNK), dtype=bool))

    def step(carry, inp):
        c_state, n_state, m_state = carry
        qb, kb, vb, li, lf = inp
        b = jnp.cumsum(lf, axis=-1)
        g = b[..., -1]
        d_intra = b[..., :, None] - b[..., None, :] + li[..., None, :]
        d_intra = jnp.where(causal, d_intra, -jnp.inf)
        d_inter = b + m_state[..., None]
        m_row = jnp.maximum(d_inter, jnp.max(d_intra, axis=-1))
        w_intra = jnp.exp(d_intra - m_row[..., None])
        w_inter = jnp.exp(d_inter - m_row)
        scores = jnp.einsum('bhld,bhsd->bhls', qb, kb) * w_intra
        num = jnp.einsum('bhls,bhsv->bhlv', scores, vb) \
            + w_inter[..., None] * jnp.einsum('bhld,bhdv->bhlv', qb, c_state)
        den = jnp.sum(scores, axis=-1) + w_inter * jnp.einsum('bhld,bhd->bhl', qb, n_state)
        h_out = num / jnp.maximum(jnp.abs(den), jnp.exp(-m_row))[..., None]
        d_state = g[..., None] - b + li
        m_new = jnp.maximum(g + m_state, jnp.max(d_state, axis=-1))
        ws = jnp.exp(d_state - m_new[..., None])
        keep = jnp.exp(g + m_state - m_new)
        c_new = keep[..., None, None] * c_state + jnp.einsum('bhs,bhsd,bhsv->bhdv', ws, kb, vb)
        n_new = keep[..., None] * n_state + jnp.einsum('bhs,bhsd->bhd', ws, kb)
        return (c_new, n_new, m_new), h_out

    init = (jnp.zeros((bsz, h, dk, dv), f32), jnp.zeros((bsz, h, dk), f32), jnp.zeros((bsz, h), f32))
    _, hs = lax.scan(step, init, (qc, kc, vc, log_i, log_f))
    hs = jnp.moveaxis(hs, (0, 2), (1, 3))
    return hs.reshape(bsz, s, h, dv)


def causal_depthwise_conv(u, w):
    ch = u.shape[-1]
    return lax.conv_general_dilated(
        u, w[:, None, :].astype(u.dtype), window_strides=(1,), padding=[(CONV_K - 1, 0)],
        dimension_numbers=('NWC', 'WIO', 'NWC'), feature_group_count=ch)


def setup_inputs(seed: int = 0) -> dict:
    key = jax.random.key(seed)
    ks = jax.random.split(key, 16)
    d = D_MODEL
    nrm = jax.random.normal
    x = nrm(ks[0], (BATCH, SEQ, d), jnp.float32)
    c = nrm(ks[1], (BATCH, d), jnp.float32)
    norm1_w = 1.0 + 0.05 * nrm(ks[2], (DEPTH, d), jnp.float32)
    w_ada = 0.3 * d ** -0.5 * nrm(ks[3], (DEPTH, d, 3 * d), jnp.float32)
    b_ada = 0.02 * nrm(ks[4], (DEPTH, 3 * d), jnp.float32)
    w_in = d ** -0.5 * nrm(ks[5], (DEPTH, d, IN_WIDTH), jnp.float32)
    b_in = 0.02 * nrm(ks[6], (DEPTH, IN_WIDTH), jnp.float32)
    offs = _split_offsets()
    f_off = offs[6]
    b_in = b_in.at[:, f_off:f_off + MLSTM_HEADS].add(jnp.linspace(3.0, 6.0, MLSTM_HEADS, dtype=jnp.float32))
    conv_w = CONV_K ** -0.5 * nrm(ks[7], (DEPTH, CONV_K, CONV_WIDTH), jnp.float32)
    headnorm_w = 1.0 + 0.05 * nrm(ks[8], (DEPTH, MLSTM_V_WIDTH), jnp.float32)
    w_proj_a = MLSTM_V_WIDTH ** -0.5 * nrm(ks[9], (DEPTH, MLSTM_V_WIDTH, d), jnp.float32)
    w_proj_b = CONV_WIDTH ** -0.5 * nrm(ks[10], (DEPTH, CONV_WIDTH, d), jnp.float32)
    w_out = d ** -0.5 * nrm(ks[11], (DEPTH, d, d), jnp.float32)
    normf_w = 1.0 + 0.05 * nrm(ks[12], (d,), jnp.float32)
    return {"x": x, "c": c, "norm1_w": norm1_w, "w_ada": w_ada, "b_ada": b_ada,
            "w_in": w_in, "b_in": b_in, "conv_w": conv_w, "headnorm_w": headnorm_w,
            "w_proj_a": w_proj_a, "w_proj_b": w_proj_b, "w_out": w_out, "normf_w": normf_w}


def reference(x, c, norm1_w, w_ada, b_ada, w_in, b_in, conv_w, headnorm_w,
              w_proj_a, w_proj_b, w_out, normf_w):
    bsz, s, d = x.shape
    offs = _split_offsets()
    split_points = offs[1:]
    c_act = jax.nn.silu(c)
    for l in range(DEPTH):
        mod = c_act @ w_ada[l] + b_ada[l]
        shift, scale, gate = jnp.split(mod, 3, axis=-1)
        h = rms_norm(x, norm1_w[l]) * (1.0 + scale[:, None, :]) + shift[:, None, :]
        proj = h @ w_in[l] + b_in[l]
        (q, k, v, o, z_a, i_pre, f_pre, u, b_gate, c_gate, z_b, g_a, g_b) = \
            jnp.split(proj, split_points, axis=-1)
        h_a = mlstm_chunkwise(
            q.reshape(bsz, s, MLSTM_HEADS, MLSTM_QK_DIM),
            k.reshape(bsz, s, MLSTM_HEADS, MLSTM_QK_DIM),
            v.reshape(bsz, s, MLSTM_HEADS, MLSTM_V_DIM),
            i_pre, f_pre)
        h_a = jax.nn.sigmoid(o.astype(jnp.float32)).reshape(bsz, s, MLSTM_HEADS, MLSTM_V_DIM) * h_a
        h_a = rms_norm(h_a, headnorm_w[l].reshape(MLSTM_HEADS, MLSTM_V_DIM))
        h_a = h_a.reshape(bsz, s, MLSTM_V_WIDTH).astype(x.dtype) * jax.nn.silu(z_a)
        y_a = h_a @ w_proj_a[l]
        h_b = b_gate * causal_depthwise_conv(c_gate * u, conv_w[l])
        y_b = (h_b * jax.nn.silu(z_b)) @ w_proj_b[l]
        merged = jax.nn.sigmoid(g_a) * y_a + jax.nn.sigmoid(g_b) * y_b
        x = x + gate[:, None, :] * (merged @ w_out[l])
    return rms_norm(x, normf_w)
```

```python
import functools
import math

import jax
import jax.numpy as jnp
from jax import lax
from jax.experimental import pallas as pl
from jax.experimental.pallas import tpu as pltpu

HEADS = 8
QK_DIM = 128
V_DIM = 256
QK_WIDTH = HEADS * QK_DIM
V_WIDTH = HEADS * V_DIM
CONV_K = 3
EPS = 1e-6

LANES = 128
GATE_PAD = 2 * LANES

ADA_TN = 768
PROJ_TM = 1024
PROJ_TN = 1024
MLSTM_L = 256
MERGE_TM = 256

VMEM_LIMIT = 56 * 1024 * 1024


def _silu(v):
    return v * jax.nn.sigmoid(v)


def _ada_kernel(c_ref, w_ref, b_ref, o_ref):
    c_act = _silu(c_ref[...])
    o_ref[...] = jnp.dot(c_act, w_ref[...], preferred_element_type=jnp.float32) + b_ref[...]


def _ada_call(c_pad, w_ada, b_ada):
    rows, d = c_pad.shape
    n = w_ada.shape[1]
    return pl.pallas_call(
        _ada_kernel,
        grid=(n // ADA_TN,),
        in_specs=[
            pl.BlockSpec((rows, d), lambda j: (0, 0)),
            pl.BlockSpec((d, ADA_TN), lambda j: (0, j)),
            pl.BlockSpec((1, ADA_TN), lambda j: (0, j)),
        ],
        out_specs=pl.BlockSpec((rows, ADA_TN), lambda j: (0, j)),
        out_shape=jax.ShapeDtypeStruct((rows, n), jnp.float32),
        compiler_params=pltpu.CompilerParams(
            dimension_semantics=("arbitrary",), vmem_limit_bytes=VMEM_LIMIT),
        name="adaln_mod",
    )(c_pad, w_ada, b_ada)


def _proj_kernel(x_ref, scale_ref, shift_ref, nw_ref, w_ref, b_ref, wg_ref, bg_ref,
                 o_ref, g_ref, h_scr):
    @pl.when(pl.program_id(2) == 0)
    def _():
        xf = x_ref[0]
        ms = jnp.mean(xf * xf, axis=-1, keepdims=True)
        y = xf * lax.rsqrt(ms + EPS) * nw_ref[...]
        h = (y * (1.0 + scale_ref[0]) + shift_ref[0]).astype(jnp.bfloat16)
        h_scr[...] = h
        g_ref[0] = jnp.dot(h, wg_ref[...], preferred_element_type=jnp.float32) + bg_ref[...]

    acc = jnp.dot(h_scr[...], w_ref[...], preferred_element_type=jnp.float32)
    o_ref[0] = (acc + b_ref[...]).astype(o_ref.dtype)


def _proj_call(x, scale, shift, norm_w, w_main, b_main, w_gate, b_gate):
    bsz, s, d = x.shape
    n = w_main.shape[1]
    grid = (bsz, s // PROJ_TM, n // PROJ_TN)
    return pl.pallas_call(
        _proj_kernel,
        grid=grid,
        in_specs=[
            pl.BlockSpec((1, PROJ_TM, d), lambda b, m, j: (b, m, 0)),
            pl.BlockSpec((1, 1, d), lambda b, m, j: (b, 0, 0)),
            pl.BlockSpec((1, 1, d), lambda b, m, j: (b, 0, 0)),
            pl.BlockSpec((1, d), lambda b, m, j: (0, 0)),
            pl.BlockSpec((d, PROJ_TN), lambda b, m, j: (0, j)),
            pl.BlockSpec((1, PROJ_TN), lambda b, m, j: (0, j)),
            pl.BlockSpec((d, GATE_PAD), lambda b, m, j: (0, 0)),
            pl.BlockSpec((1, GATE_PAD), lambda b, m, j: (0, 0)),
        ],
        out_specs=[
            pl.BlockSpec((1, PROJ_TM, PROJ_TN), lambda b, m, j: (b, m, j)),
            pl.BlockSpec((1, PROJ_TM, GATE_PAD), lambda b, m, j: (b, m, 0)),
        ],
        out_shape=[
            jax.ShapeDtypeStruct((bsz, s, n), jnp.bfloat16),
            jax.ShapeDtypeStruct((bsz, s, GATE_PAD), jnp.float32),
        ],
        scratch_shapes=[pltpu.VMEM((PROJ_TM, d), jnp.bfloat16)],
        compiler_params=pltpu.CompilerParams(
            dimension_semantics=("arbitrary", "arbitrary", "arbitrary"),
            vmem_limit_bytes=VMEM_LIMIT),
        name="in_proj",
    )(x, scale, shift, norm_w, w_main, b_main, w_gate, b_gate)


def _prefix_scan_rows(v, combine, fill):
    n = v.shape[0]
    row = lax.broadcasted_iota(jnp.int32, v.shape, 0)
    step = 1
    while step < n:
        shifted = pltpu.roll(v, step, 0)
        v = combine(v, jnp.where(row >= step, shifted, fill))
        step *= 2
    return v


def _mlstm_kernel(q_ref, k_ref, v_ref, o_ref, z_ref, g_ref, hw_ref, out_ref,
                  c_scr, n_scr, m_scr):
    L = MLSTM_L

    @pl.when(pl.program_id(1) == 0)
    def _():
        c_scr[...] = jnp.zeros_like(c_scr)
        n_scr[...] = jnp.zeros_like(n_scr)
        m_scr[...] = jnp.zeros_like(m_scr)

    gates = g_ref[0]
    li = gates[:, :LANES]
    lf = jax.nn.log_sigmoid(gates[:, LANES:])
    b = _prefix_scan_rows(lf, jnp.add, 0.0)
    a = li - b
    cm = _prefix_scan_rows(a, jnp.maximum, -jnp.inf)
    m_state = m_scr[...]
    big_m = jnp.maximum(m_state, cm)
    w_inter = jnp.exp(m_state - big_m) * (QK_DIM ** -0.5)
    clamp = jnp.exp(-b - big_m)
    m_last = big_m[L - 1:L, :]
    ws = jnp.exp(a - m_last)
    keep = jnp.exp(m_state - m_last)
    m_scr[...] = b[L - 1:L, :] + m_last
    a_t = a.T
    neg_m = math.log(QK_DIM ** -0.5) - big_m

    ri = lax.broadcasted_iota(jnp.int32, (L, L), 0)
    ci = lax.broadcasted_iota(jnp.int32, (L, L), 1)
    causal = ci <= ri

    for h in range(HEADS):
        qh = q_ref[0, :, h * QK_DIM:(h + 1) * QK_DIM]
        kh = k_ref[0, :, h * QK_DIM:(h + 1) * QK_DIM]
        vh = v_ref[0, :, h * V_DIM:(h + 1) * V_DIM]
        c_h = c_scr[h]
        n_h = n_scr[h:h + 1, :]

        qk = lax.dot_general(qh, kh, (((1,), (1,)), ((), ())),
                             preferred_element_type=jnp.float32)
        expo = a_t[h:h + 1, :] + neg_m[:, h:h + 1]
        scores = jnp.where(causal, jnp.exp(expo), 0.0) * qk
        wi = w_inter[:, h:h + 1]
        num = jnp.dot(scores.astype(jnp.bfloat16), vh, preferred_element_type=jnp.float32)
        num = num + wi * jnp.dot(qh, c_h.astype(jnp.bfloat16),
                                 preferred_element_type=jnp.float32)
        qf = qh.astype(jnp.float32)
        den = jnp.sum(scores, axis=-1, keepdims=True) \
            + wi * jnp.sum(qf * n_h, axis=-1, keepdims=True)
        h_out = num / jnp.maximum(jnp.abs(den), clamp[:, h:h + 1])

        og = o_ref[0, :, h * V_DIM:(h + 1) * V_DIM].astype(jnp.float32)
        hg = jax.nn.sigmoid(og) * h_out
        ms = jnp.mean(hg * hg, axis=-1, keepdims=True)
        hn = hg * lax.rsqrt(ms + EPS) * hw_ref[:, h * V_DIM:(h + 1) * V_DIM]
        zg = z_ref[0, :, h * V_DIM:(h + 1) * V_DIM].astype(jnp.float32)
        out_ref[0, :, h * V_DIM:(h + 1) * V_DIM] = (hn * _silu(zg)).astype(out_ref.dtype)

        ws_h = ws[:, h:h + 1]
        keep_h = keep[:, h:h + 1]
        vw = (ws_h * vh.astype(jnp.float32)).astype(jnp.bfloat16)
        kv = lax.dot_general(kh, vw, (((0,), (0,)), ((), ())),
                             preferred_element_type=jnp.float32)
        c_scr[h] = keep_h * c_h + kv
        n_scr[h:h + 1, :] = keep_h * n_h + jnp.sum(ws_h * kh.astype(jnp.float32),
                                                    axis=0, keepdims=True)


def _mlstm_call(proj, gates, headnorm_w):
    bsz, s, _ = proj.shape
    L = MLSTM_L
    qk_blk = lambda off: pl.BlockSpec((1, L, QK_WIDTH), lambda b, c: (b, c, off))
    v_blk = lambda off: pl.BlockSpec((1, L, V_WIDTH), lambda b, c: (b, c, off))
    return pl.pallas_call(
        _mlstm_kernel,
        grid=(bsz, s // L),
        in_specs=[
            qk_blk(0), qk_blk(1),
            v_blk(1), v_blk(2), v_blk(3),
            pl.BlockSpec((1, L, GATE_PAD), lambda b, c: (b, c, 0)),
            pl.BlockSpec((1, V_WIDTH), lambda b, c: (0, 0)),
        ],
        out_specs=pl.BlockSpec((1, L, V_WIDTH), lambda b, c: (b, c, 0)),
        out_shape=jax.ShapeDtypeStruct((bsz, s, V_WIDTH), jnp.bfloat16),
        scratch_shapes=[
            pltpu.VMEM((HEADS, QK_DIM, V_DIM), jnp.float32),
            pltpu.VMEM((HEADS, QK_DIM), jnp.float32),
            pltpu.VMEM((1, LANES), jnp.float32),
        ],
        compiler_params=pltpu.CompilerParams(
            dimension_semantics=("arbitrary", "arbitrary"), vmem_limit_bytes=VMEM_LIMIT),
        name="mlstm",
    )(proj, proj, proj, proj, proj, gates, headnorm_w)


def _merge_kernel(ha_ref, u_ref, bg_ref, cg_ref, zb_ref, ga_ref, gb_ref, x_ref, gate_ref,
                  cw_ref, wa_ref, wb_ref, wo_ref, nf_ref, out_ref, carry_scr, *, final_norm):
    tm = MERGE_TM

    @pl.when(pl.program_id(1) == 0)
    def _():
        carry_scr[...] = jnp.zeros_like(carry_scr)

    cu = cg_ref[0].astype(jnp.float32) * u_ref[0].astype(jnp.float32)
    prev = carry_scr[...]
    row = lax.broadcasted_iota(jnp.int32, cu.shape, 0)
    cu1 = jnp.where(row >= 1, pltpu.roll(cu, 1, 0), prev[7:8, :])
    cu2 = jnp.where(row >= 2, pltpu.roll(cu, 2, 0),
                    jnp.where(row == 1, prev[7:8, :], prev[6:7, :]))
    carry_scr[...] = cu[tm - 8:, :]
    conv = cw_ref[0:1, :] * cu2 + cw_ref[1:2, :] * cu1 + cw_ref[2:3, :] * cu
    hb = bg_ref[0].astype(jnp.float32) * conv * _silu(zb_ref[0].astype(jnp.float32))
    y_b = jnp.dot(hb.astype(jnp.bfloat16), wb_ref[...], preferred_element_type=jnp.float32)
    y_a = jnp.dot(ha_ref[0], wa_ref[...], preferred_element_type=jnp.float32)
    merged = jax.nn.sigmoid(ga_ref[0].astype(jnp.float32)) * y_a \
        + jax.nn.sigmoid(gb_ref[0].astype(jnp.float32)) * y_b
    o = jnp.dot(merged.astype(jnp.bfloat16), wo_ref[...], preferred_element_type=jnp.float32)
    xo = x_ref[0] + gate_ref[0] * o
    if final_norm:
        ms = jnp.mean(xo * xo, axis=-1, keepdims=True)
        xo = xo * lax.rsqrt(ms + EPS) * nf_ref[...]
    out_ref[0] = xo


def _merge_call(h_a, proj, x, gate, conv_w, w_proj_a, w_proj_b, w_out, normf_w, final_norm):
    bsz, s, d = x.shape
    tm = MERGE_TM
    col = lambda off: pl.BlockSpec((1, tm, d), lambda b, t: (b, t, off))
    const = lambda shape: pl.BlockSpec(shape, lambda b, t: (0,) * len(shape),
                                       pipeline_mode=pl.Buffered(1))
    return pl.pallas_call(
        functools.partial(_merge_kernel, final_norm=final_norm),
        grid=(bsz, s // tm),
        in_specs=[
            col(0),
            col(4), col(5), col(6), col(7), col(8), col(9),
            col(0),
            pl.BlockSpec((1, 1, d), lambda b, t: (b, 0, 0)),
            const((CONV_K, d)),
            const((V_WIDTH, d)), const((d, d)), const((d, d)),
            const((1, d)),
        ],
        out_specs=pl.BlockSpec((1, tm, d), lambda b, t: (b, t, 0)),
        out_shape=jax.ShapeDtypeStruct((bsz, s, d), jnp.float32),
        scratch_shapes=[pltpu.VMEM((8, d), jnp.float32)],
        compiler_params=pltpu.CompilerParams(
            dimension_semantics=("arbitrary", "arbitrary"), vmem_limit_bytes=VMEM_LIMIT),
        name="conv_merge_out",
    )(h_a, proj, proj, proj, proj, proj, proj, x, gate, conv_w, w_proj_a, w_proj_b, w_out, normf_w)


def kernel(x, c, norm1_w, w_ada, b_ada, w_in, b_in, conv_w, headnorm_w,
           w_proj_a, w_proj_b, w_out, normf_w):
    bsz, s, d = x.shape
    depth = norm1_w.shape[0]
    gate_off = 2 * QK_WIDTH + 3 * V_WIDTH
    bf16 = jnp.bfloat16

    c_pad = jnp.pad(c, ((0, 8 - bsz), (0, 0)))
    for l in range(depth):
        mod = _ada_call(c_pad, w_ada[l], b_ada[l][None, :])[:bsz]
        shift, scale, gate = (m[:, None, :] for m in jnp.split(mod, 3, axis=-1))

        w_l, b_l = w_in[l], b_in[l]
        w_main = jnp.concatenate([w_l[:, :gate_off], w_l[:, gate_off + 2 * HEADS:]], axis=1).astype(bf16)
        b_main = jnp.concatenate([b_l[:gate_off], b_l[gate_off + 2 * HEADS:]])[None, :]
        w_gate = jnp.zeros((d, GATE_PAD), jnp.float32)
        w_gate = w_gate.at[:, :HEADS].set(w_l[:, gate_off:gate_off + HEADS])
        w_gate = w_gate.at[:, LANES:LANES + HEADS].set(w_l[:, gate_off + HEADS:gate_off + 2 * HEADS])
        b_gate = jnp.zeros((GATE_PAD,), jnp.float32)
        b_gate = b_gate.at[:HEADS].set(b_l[gate_off:gate_off + HEADS])
        b_gate = b_gate.at[LANES:LANES + HEADS].set(b_l[gate_off + HEADS:gate_off + 2 * HEADS])

        proj, gates = _proj_call(x, scale, shift, norm1_w[l][None, :], w_main, b_main,
                                 w_gate.astype(bf16), b_gate[None, :])
        h_a = _mlstm_call(proj, gates, headnorm_w[l][None, :])
        x = _merge_call(h_a, proj, x, gate, conv_w[l], w_proj_a[l].astype(bf16),
                        w_proj_b[l].astype(bf16), w_out[l].astype(bf16), normf_w[None, :],
                        final_norm=(l == depth - 1))
    return x
```

```python
import functools
import math

import jax
import jax.numpy as jnp
from jax import lax
from jax.experimental import pallas as pl
from jax.experimental.pallas import tpu as pltpu

HEADS = 8
QK_DIM = 128
V_DIM = 256
QK_WIDTH = HEADS * QK_DIM
V_WIDTH = HEADS * V_DIM
MLSTM_WIDTH = 2 * QK_WIDTH + 3 * V_WIDTH
CONV_K = 3
EPS = 1e-6

LANES = 128
GATE_PAD = 2 * LANES

ADA_TN = 768
PROJ_TM = 1024
PROJA_TN = 1024
MLSTM_L = 256
MERGE_TM = 256

VMEM_LIMIT = 56 * 1024 * 1024


def _sigmoid(v):
    return 0.5 * jnp.tanh(0.5 * v) + 0.5


def _silu(v):
    return v * _sigmoid(v)


def _ada_kernel(c_ref, w_ref, b_ref, o_ref):
    c_act = _silu(c_ref[...])
    o_ref[...] = jnp.dot(c_act, w_ref[...], preferred_element_type=jnp.float32) + b_ref[...]


def _ada_call(c_pad, w_ada, b_ada):
    rows, d = c_pad.shape
    n = w_ada.shape[1]
    return pl.pallas_call(
        _ada_kernel,
        grid=(n // ADA_TN,),
        in_specs=[
            pl.BlockSpec((rows, d), lambda j: (0, 0)),
            pl.BlockSpec((d, ADA_TN), lambda j: (0, j)),
            pl.BlockSpec((1, ADA_TN), lambda j: (0, j)),
        ],
        out_specs=pl.BlockSpec((rows, ADA_TN), lambda j: (0, j)),
        out_shape=jax.ShapeDtypeStruct((rows, n), jnp.float32),
        compiler_params=pltpu.CompilerParams(
            dimension_semantics=("arbitrary",), vmem_limit_bytes=VMEM_LIMIT),
        name="adaln_mod",
    )(c_pad, w_ada, b_ada)


def _proj_a_kernel(x_ref, scale_ref, shift_ref, nw_ref, w_ref, b_ref, wg_ref, bg_ref,
                   o_ref, g_ref, h_ref):
    @pl.when(pl.program_id(2) == 0)
    def _():
        xf = x_ref[0]
        ms = jnp.mean(xf * xf, axis=-1, keepdims=True)
        y = xf * lax.rsqrt(ms + EPS) * nw_ref[...]
        h = (y * (1.0 + scale_ref[0]) + shift_ref[0]).astype(jnp.bfloat16)
        h_ref[0] = h
        g_ref[0] = jnp.dot(h, wg_ref[...], preferred_element_type=jnp.float32) + bg_ref[...]

    acc = jnp.dot(h_ref[0], w_ref[...], preferred_element_type=jnp.float32)
    o_ref[0] = (acc + b_ref[...]).astype(o_ref.dtype)


def _proj_a_call(x, scale, shift, norm_w, w_a, b_a, w_gate, b_gate):
    bsz, s, d = x.shape
    n = w_a.shape[1]
    tm, tn = PROJ_TM, PROJA_TN
    return pl.pallas_call(
        _proj_a_kernel,
        grid=(bsz, s // tm, n // tn),
        in_specs=[
            pl.BlockSpec((1, tm, d), lambda b, m, j: (b, m, 0)),
            pl.BlockSpec((1, 1, d), lambda b, m, j: (b, 0, 0)),
            pl.BlockSpec((1, 1, d), lambda b, m, j: (b, 0, 0)),
            pl.BlockSpec((1, d), lambda b, m, j: (0, 0)),
            pl.BlockSpec((d, tn), lambda b, m, j: (0, j)),
            pl.BlockSpec((1, tn), lambda b, m, j: (0, j)),
            pl.BlockSpec((d, GATE_PAD), lambda b, m, j: (0, 0)),
            pl.BlockSpec((1, GATE_PAD), lambda b, m, j: (0, 0)),
        ],
        out_specs=[
            pl.BlockSpec((1, tm, tn), lambda b, m, j: (b, m, j)),
            pl.BlockSpec((1, tm, GATE_PAD), lambda b, m, j: (b, m, 0)),
            pl.BlockSpec((1, tm, d), lambda b, m, j: (b, m, 0)),
        ],
        out_shape=[
            jax.ShapeDtypeStruct((bsz, s, n), jnp.bfloat16),
            jax.ShapeDtypeStruct((bsz, s, GATE_PAD), jnp.float32),
            jax.ShapeDtypeStruct((bsz, s, d), jnp.bfloat16),
        ],
        compiler_params=pltpu.CompilerParams(
            dimension_semantics=("arbitrary", "arbitrary", "arbitrary"),
            vmem_limit_bytes=VMEM_LIMIT),
        name="proj_mlstm",
    )(x, scale, shift, norm_w, w_a, b_a, w_gate, b_gate)


def _prefix_scan_rows(v, combine, fill):
    n = v.shape[0]
    row = lax.broadcasted_iota(jnp.int32, v.shape, 0)
    step = 1
    while step < n:
        shifted = pltpu.roll(v, step, 0)
        v = combine(v, jnp.where(row >= step, shifted, fill))
        step *= 2
    return v


def _mlstm_chunk(qh, kh, vh, og, zg, li, fpre, hw, c_scr, n_scr, m_scr):
    L = qh.shape[0]
    lf = jax.nn.log_sigmoid(fpre)
    b = _prefix_scan_rows(lf, jnp.add, 0.0)
    a = li - b
    cm = _prefix_scan_rows(a, jnp.maximum, -jnp.inf)
    m_state = m_scr[...]
    big_m = jnp.maximum(m_state, cm)
    w_inter = jnp.exp(m_state - big_m) * (QK_DIM ** -0.5)
    clamp = jnp.exp(-b - big_m)
    m_last = big_m[L - 1:L, :]
    ws = jnp.exp(a - m_last)
    keep = jnp.exp(m_state - m_last)
    m_scr[...] = b[L - 1:L, :] + m_last
    a_row = jnp.broadcast_to(a, (L, LANES)).T[0:1, :]
    neg_m = math.log(QK_DIM ** -0.5) - big_m

    ri = lax.broadcasted_iota(jnp.int32, (L, L), 0)
    ci = lax.broadcasted_iota(jnp.int32, (L, L), 1)
    c_h = c_scr[...]
    n_h = n_scr[...]

    qk = lax.dot_general(qh, kh, (((1,), (1,)), ((), ())), preferred_element_type=jnp.float32)
    scores = jnp.where(ci <= ri, jnp.exp(a_row + neg_m), 0.0) * qk
    num = jnp.dot(scores.astype(jnp.bfloat16), vh, preferred_element_type=jnp.float32)
    num = num + w_inter * jnp.dot(qh, c_h.astype(jnp.bfloat16), preferred_element_type=jnp.float32)
    den = jnp.sum(scores, axis=-1, keepdims=True) \
        + w_inter * jnp.sum(qh.astype(jnp.float32) * n_h, axis=-1, keepdims=True)
    h_out = num * (1.0 / jnp.maximum(jnp.abs(den), clamp))

    vw = (ws * vh.astype(jnp.float32)).astype(jnp.bfloat16)
    kv = lax.dot_general(kh, vw, (((0,), (0,)), ((), ())), preferred_element_type=jnp.float32)
    c_scr[...] = keep * c_h + kv
    n_scr[...] = keep * n_h + jnp.sum(ws * kh.astype(jnp.float32), axis=0, keepdims=True)

    hg = _sigmoid(og.astype(jnp.float32)) * h_out
    ms = jnp.mean(hg * hg, axis=-1, keepdims=True)
    return hg * lax.rsqrt(ms + EPS) * hw * _silu(zg.astype(jnp.float32))


def _proj_b_kernel(h_ref, w_ref, b_ref, q_ref, k_ref, v_ref, o_ref, z_ref, g_ref, hw_ref,
                   p_ref, ha_ref, c_scr, n_scr, m_scr):
    head = pl.program_id(0)

    @pl.when(pl.program_id(2) == 0)
    def _():
        c_scr[...] = jnp.zeros_like(c_scr)
        n_scr[...] = jnp.zeros_like(n_scr)
        m_scr[...] = jnp.zeros_like(m_scr)

    acc = jnp.dot(h_ref[0], w_ref[...], preferred_element_type=jnp.float32)
    p_ref[0] = (acc + b_ref[...]).astype(p_ref.dtype)

    L = MLSTM_L
    lane = lax.broadcasted_iota(jnp.int32, (L, LANES), 1)
    for cc in range(PROJ_TM // L):
        rows = slice(cc * L, (cc + 1) * L)
        gates = g_ref[0, rows, :]
        li = jnp.sum(jnp.where(lane == head, gates[:, :LANES], 0.0), axis=-1, keepdims=True)
        fpre = jnp.sum(jnp.where(lane == head, gates[:, LANES:], 0.0), axis=-1, keepdims=True)
        out = _mlstm_chunk(q_ref[0, rows, :], k_ref[0, rows, :], v_ref[0, rows, :],
                           o_ref[0, rows, :], z_ref[0, rows, :], li, fpre, hw_ref[...],
                           c_scr, n_scr, m_scr)
        ha_ref[0, rows, :] = out.astype(ha_ref.dtype)


def _proj_b_call(h, w_b, b_b, proj_a, gates, headnorm_w):
    bsz, s, d = h.shape
    n = w_b.shape[1]
    tm = PROJ_TM
    tn = n // HEADS
    qk_blk = lambda off: pl.BlockSpec((1, tm, QK_DIM), lambda hd, b, m: (b, m, off + hd))
    v_blk = lambda off: pl.BlockSpec((1, tm, V_DIM), lambda hd, b, m: (b, m, off + hd))
    return pl.pallas_call(
        _proj_b_kernel,
        grid=(HEADS, bsz, s // tm),
        in_specs=[
            pl.BlockSpec((1, tm, d), lambda hd, b, m: (b, m, 0)),
            pl.BlockSpec((d, tn), lambda hd, b, m: (0, hd)),
            pl.BlockSpec((1, tn), lambda hd, b, m: (0, hd)),
            qk_blk(0), qk_blk(HEADS),
            v_blk(HEADS), v_blk(2 * HEADS), v_blk(3 * HEADS),
            pl.BlockSpec((1, tm, GATE_PAD), lambda hd, b, m: (b, m, 0)),
            pl.BlockSpec((1, V_DIM), lambda hd, b, m: (0, hd)),
        ],
        out_specs=[
            pl.BlockSpec((1, tm, tn), lambda hd, b, m: (b, m, hd)),
            pl.BlockSpec((1, tm, V_DIM), lambda hd, b, m: (b, m, hd)),
        ],
        out_shape=[
            jax.ShapeDtypeStruct((bsz, s, n), jnp.bfloat16),
            jax.ShapeDtypeStruct((bsz, s, V_WIDTH), jnp.bfloat16),
        ],
        scratch_shapes=[
            pltpu.VMEM((QK_DIM, V_DIM), jnp.float32),
            pltpu.VMEM((1, QK_DIM), jnp.float32),
            pltpu.VMEM((1, 1), jnp.float32),
        ],
        compiler_params=pltpu.CompilerParams(
            dimension_semantics=("arbitrary", "arbitrary", "arbitrary"),
            vmem_limit_bytes=VMEM_LIMIT),
        name="proj_conv_mlstm",
    )(h, w_b, b_b, proj_a, proj_a, proj_a, proj_a, proj_a, gates, headnorm_w)


def _merge_kernel(ha_ref, u_ref, bg_ref, cg_ref, zb_ref, ga_ref, gb_ref, x_ref, gate_ref,
                  cw_ref, wa_ref, wb_ref, wo_ref, nf_ref, out_ref, carry_scr, *, final_norm):
    tm = MERGE_TM

    @pl.when(pl.program_id(1) == 0)
    def _():
        carry_scr[...] = jnp.zeros_like(carry_scr)

    cu = cg_ref[0].astype(jnp.float32) * u_ref[0].astype(jnp.float32)
    prev = carry_scr[...]
    row = lax.broadcasted_iota(jnp.int32, cu.shape, 0)
    cu1 = jnp.where(row >= 1, pltpu.roll(cu, 1, 0), prev[7:8, :])
    cu2 = jnp.where(row >= 2, pltpu.roll(cu, 2, 0),
                    jnp.where(row == 1, prev[7:8, :], prev[6:7, :]))
    carry_scr[...] = cu[tm - 8:, :]
    conv = cw_ref[0:1, :] * cu2 + cw_ref[1:2, :] * cu1 + cw_ref[2:3, :] * cu
    hb = bg_ref[0].astype(jnp.float32) * conv * _silu(zb_ref[0].astype(jnp.float32))
    y_b = jnp.dot(hb.astype(jnp.bfloat16), wb_ref[...], preferred_element_type=jnp.float32)
    y_a = jnp.dot(ha_ref[0], wa_ref[...], preferred_element_type=jnp.float32)
    merged = _sigmoid(ga_ref[0].astype(jnp.float32)) * y_a \
        + _sigmoid(gb_ref[0].astype(jnp.float32)) * y_b
    o = jnp.dot(merged.astype(jnp.bfloat16), wo_ref[...], preferred_element_type=jnp.float32)
    xo = x_ref[0] + gate_ref[0] * o
    if final_norm:
        ms = jnp.mean(xo * xo, axis=-1, keepdims=True)
        xo = xo * lax.rsqrt(ms + EPS) * nf_ref[...]
    out_ref[0] = xo


def _merge_call(h_a, proj_b, x, gate, conv_w, w_proj_a, w_proj_b, w_out, normf_w, final_norm):
    bsz, s, d = x.shape
    tm = MERGE_TM
    col = lambda off: pl.BlockSpec((1, tm, d), lambda b, t: (b, t, off))
    const = lambda shape: pl.BlockSpec(shape, lambda b, t: (0,) * len(shape),
                                       pipeline_mode=pl.Buffered(1))
    return pl.pallas_call(
        functools.partial(_merge_kernel, final_norm=final_norm),
        grid=(bsz, s // tm),
        in_specs=[
            col(0),
            col(0), col(1), col(2), col(3), col(4), col(5),
            col(0),
            pl.BlockSpec((1, 1, d), lambda b, t: (b, 0, 0)),
            const((CONV_K, d)),
            const((V_WIDTH, d)), const((d, d)), const((d, d)),
            const((1, d)),
        ],
        out_specs=pl.BlockSpec((1, tm, d), lambda b, t: (b, t, 0)),
        out_shape=jax.ShapeDtypeStruct((bsz, s, d), jnp.float32),
        scratch_shapes=[pltpu.VMEM((8, d), jnp.float32)],
        compiler_params=pltpu.CompilerParams(
            dimension_semantics=("arbitrary", "arbitrary"), vmem_limit_bytes=VMEM_LIMIT),
        name="conv_merge_out",
    )(h_a, proj_b, proj_b, proj_b, proj_b, proj_b, proj_b, x, gate, conv_w,
      w_proj_a, w_proj_b, w_out, normf_w)


def kernel(x, c, norm1_w, w_ada, b_ada, w_in, b_in, conv_w, headnorm_w,
           w_proj_a, w_proj_b, w_out, normf_w):
    bsz, s, d = x.shape
    depth = norm1_w.shape[0]
    gate_off = MLSTM_WIDTH
    rest_off = gate_off + 2 * HEADS
    bf16 = jnp.bfloat16

    c_pad = jnp.pad(c, ((0, 8 - bsz), (0, 0)))
    for l in range(depth):
        mod = _ada_call(c_pad, w_ada[l], b_ada[l][None, :])[:bsz]
        shift, scale, gate = (m[:, None, :] for m in jnp.split(mod, 3, axis=-1))

        w_l, b_l = w_in[l], b_in[l]
        w_a = w_l[:, :gate_off].astype(bf16)
        w_b = w_l[:, rest_off:].astype(bf16)
        w_gate = jnp.zeros((d, GATE_PAD), bf16)
        w_gate = w_gate.at[:, :HEADS].set(w_l[:, gate_off:gate_off + HEADS].astype(bf16))
        w_gate = w_gate.at[:, LANES:LANES + HEADS].set(w_l[:, gate_off + HEADS:rest_off].astype(bf16))
        b_gate = jnp.zeros((GATE_PAD,), jnp.float32)
        b_gate = b_gate.at[:HEADS].set(b_l[gate_off:gate_off + HEADS])
        b_gate = b_gate.at[LANES:LANES + HEADS].set(b_l[gate_off + HEADS:rest_off])

        proj_a, gates, h = _proj_a_call(x, scale, shift, norm1_w[l][None, :], w_a,
                                        b_l[None, :gate_off], w_gate, b_gate[None, :])
        proj_b, h_a = _proj_b_call(h, w_b, b_l[None, rest_off:], proj_a, gates, headnorm_w[l][None, :])
        x = _merge_call(h_a, proj_b, x, gate, conv_w[l], w_proj_a[l].astype(bf16),
                        w_proj_b[l].astype(bf16), w_out[l].astype(bf16), normf_w[None, :],
                        final_norm=(l == depth - 1))
    return x
```

```python
import functools
import math

import jax
import jax.numpy as jnp
from jax import lax
from jax.experimental import pallas as pl
from jax.experimental.pallas import tpu as pltpu

HEADS = 8
QK_DIM = 128
V_DIM = 256
QK_WIDTH = HEADS * QK_DIM
V_WIDTH = HEADS * V_DIM
MLSTM_WIDTH = 2 * QK_WIDTH + 3 * V_WIDTH
CONV_K = 3
EPS = 1e-6

LANES = 128
GATE_PAD = 2 * LANES

ADA_TN = 768
PROJ_TM = 1024
PROJA_TN = 1024
MLSTM_L = 256
MERGE_TM = 256
CONV_CB = V_DIM
WIN_SHIFT = (MLSTM_WIDTH + 2 * HEADS) % LANES
WIN_W = CONV_CB + LANES

VMEM_LIMIT = 56 * 1024 * 1024


def _sigmoid(v):
    return 0.5 * jnp.tanh(0.5 * v) + 0.5


def _silu(v):
    return v * _sigmoid(v)


def _ada_kernel(c_ref, w_ref, b_ref, o_ref):
    c_act = _silu(c_ref[...])
    o_ref[...] = jnp.dot(c_act, w_ref[...], preferred_element_type=jnp.float32) + b_ref[...]


def _ada_call(c_pad, w_ada, b_ada):
    rows, d = c_pad.shape
    n = w_ada.shape[1]
    return pl.pallas_call(
        _ada_kernel,
        grid=(n // ADA_TN,),
        in_specs=[
            pl.BlockSpec((rows, d), lambda j: (0, 0)),
            pl.BlockSpec((d, ADA_TN), lambda j: (0, j)),
            pl.BlockSpec((1, ADA_TN), lambda j: (0, j)),
        ],
        out_specs=pl.BlockSpec((rows, ADA_TN), lambda j: (0, j)),
        out_shape=jax.ShapeDtypeStruct((rows, n), jnp.float32),
        compiler_params=pltpu.CompilerParams(
            dimension_semantics=("arbitrary",), vmem_limit_bytes=VMEM_LIMIT),
        name="adaln_mod",
    )(c_pad, w_ada, b_ada)


def _proj_a_kernel(x_ref, scale_ref, shift_ref, nw_ref, w_ref, b_ref, wg_ref, bg_ref,
                   o_ref, g_ref, h_ref):
    @pl.when(pl.program_id(2) == 0)
    def _():
        xf = x_ref[0]
        ms = jnp.mean(xf * xf, axis=-1, keepdims=True)
        y = xf * lax.rsqrt(ms + EPS) * nw_ref[...]
        h = (y * (1.0 + scale_ref[0]) + shift_ref[0]).astype(jnp.bfloat16)
        h_ref[0] = h
        g_ref[0] = jnp.dot(h, wg_ref[...], preferred_element_type=jnp.float32) + bg_ref[...]

    acc = jnp.dot(h_ref[0], w_ref[...], preferred_element_type=jnp.float32)
    o_ref[0] = (acc + b_ref[...]).astype(o_ref.dtype)


def _proj_a_call(x, scale, shift, norm_w, w_a, b_a, w_gate, b_gate):
    bsz, s, d = x.shape
    n = w_a.shape[1]
    tm, tn = PROJ_TM, PROJA_TN
    return pl.pallas_call(
        _proj_a_kernel,
        grid=(bsz, s // tm, n // tn),
        in_specs=[
            pl.BlockSpec((1, tm, d), lambda b, m, j: (b, m, 0)),
            pl.BlockSpec((1, 1, d), lambda b, m, j: (b, 0, 0)),
            pl.BlockSpec((1, 1, d), lambda b, m, j: (b, 0, 0)),
            pl.BlockSpec((1, d), lambda b, m, j: (0, 0)),
            pl.BlockSpec((d, tn), lambda b, m, j: (0, j)),
            pl.BlockSpec((1, tn), lambda b, m, j: (0, j)),
            pl.BlockSpec((d, GATE_PAD), lambda b, m, j: (0, 0)),
            pl.BlockSpec((1, GATE_PAD), lambda b, m, j: (0, 0)),
        ],
        out_specs=[
            pl.BlockSpec((1, tm, tn), lambda b, m, j: (b, m, j)),
            pl.BlockSpec((1, tm, GATE_PAD), lambda b, m, j: (b, m, 0)),
            pl.BlockSpec((1, tm, d), lambda b, m, j: (b, m, 0)),
        ],
        out_shape=[
            jax.ShapeDtypeStruct((bsz, s, n), jnp.bfloat16),
            jax.ShapeDtypeStruct((bsz, s, GATE_PAD), jnp.float32),
            jax.ShapeDtypeStruct((bsz, s, d), jnp.bfloat16),
        ],
        compiler_params=pltpu.CompilerParams(
            dimension_semantics=("arbitrary", "arbitrary", "arbitrary"),
            vmem_limit_bytes=VMEM_LIMIT),
        name="proj_mlstm",
    )(x, scale, shift, norm_w, w_a, b_a, w_gate, b_gate)


def _prefix_scan_rows(v, combine, fill):
    n = v.shape[0]
    row = lax.broadcasted_iota(jnp.int32, v.shape, 0)
    step = 1
    while step < n:
        shifted = pltpu.roll(v, step, 0)
        v = combine(v, jnp.where(row >= step, shifted, fill))
        step *= 2
    return v


def _mlstm_chunk(qh, kh, vh, og, zg, li, fpre, hw, c_scr, n_scr, m_scr):
    L = qh.shape[0]
    lf = jax.nn.log_sigmoid(fpre)
    b = _prefix_scan_rows(lf, jnp.add, 0.0)
    a = li - b
    cm = _prefix_scan_rows(a, jnp.maximum, -jnp.inf)
    m_state = m_scr[...]
    big_m = jnp.maximum(m_state, cm)
    w_inter = jnp.exp(m_state - big_m) * (QK_DIM ** -0.5)
    clamp = jnp.exp(-b - big_m)
    m_last = big_m[L - 1:L, :]
    ws = jnp.exp(a - m_last)
    keep = jnp.exp(m_state - m_last)
    m_scr[...] = b[L - 1:L, :] + m_last
    a_row = jnp.broadcast_to(a, (L, LANES)).T[0:1, :]
    neg_m = math.log(QK_DIM ** -0.5) - big_m

    ri = lax.broadcasted_iota(jnp.int32, (L, L), 0)
    ci = lax.broadcasted_iota(jnp.int32, (L, L), 1)
    c_h = c_scr[...]
    n_h = n_scr[...]

    qk = lax.dot_general(qh, kh, (((1,), (1,)), ((), ())), preferred_element_type=jnp.float32)
    scores = jnp.where(ci <= ri, jnp.exp(a_row + neg_m), 0.0) * qk
    num = jnp.dot(scores.astype(jnp.bfloat16), vh, preferred_element_type=jnp.float32)
    num = num + w_inter * jnp.dot(qh, c_h.astype(jnp.bfloat16), preferred_element_type=jnp.float32)
    den = jnp.sum(scores, axis=-1, keepdims=True) \
        + w_inter * jnp.sum(qh.astype(jnp.float32) * n_h, axis=-1, keepdims=True)
    h_out = num * (1.0 / jnp.maximum(jnp.abs(den), clamp))

    vw = (ws * vh.astype(jnp.float32)).astype(jnp.bfloat16)
    kv = lax.dot_general(kh, vw, (((0,), (0,)), ((), ())), preferred_element_type=jnp.float32)
    c_scr[...] = keep * c_h + kv
    n_scr[...] = keep * n_h + jnp.sum(ws * kh.astype(jnp.float32), axis=0, keepdims=True)

    hg = _sigmoid(og.astype(jnp.float32)) * h_out
    ms = jnp.mean(hg * hg, axis=-1, keepdims=True)
    return hg * lax.rsqrt(ms + EPS) * hw * _silu(zg.astype(jnp.float32))


def _proj_b_kernel(h_ref, wu_ref, wbg_ref, wcg_ref, wzb_ref, wga_ref, wgb_ref, b_ref, cw_ref,
                   q_ref, k_ref, v_ref, o_ref, z_ref, g_ref, hw_ref,
                   ha_ref, hb_ref, sga_ref, sgb_ref,
                   w_scr, c_scr, n_scr, m_scr, carry_scr):
    head = pl.program_id(0)
    cb = CONV_CB

    @pl.when((pl.program_id(1) == 0) & (pl.program_id(2) == 0))
    def _():
        wins = (wu_ref, wbg_ref, wcg_ref, wzb_ref, wga_ref, wgb_ref)
        step = 512
        for g, win in enumerate(wins):
            for r in range(0, w_scr.shape[0], step):
                blk = win[r:r + step, :]
                w_scr[r:r + step, g * cb:(g + 1) * cb] = \
                    blk[:, WIN_SHIFT:WIN_SHIFT + cb].astype(w_scr.dtype)

    @pl.when(pl.program_id(2) == 0)
    def _():
        c_scr[...] = jnp.zeros_like(c_scr)
        n_scr[...] = jnp.zeros_like(n_scr)
        m_scr[...] = jnp.zeros_like(m_scr)
        carry_scr[...] = jnp.zeros_like(carry_scr)

    L = MLSTM_L
    lane = lax.broadcasted_iota(jnp.int32, (L, LANES), 1)
    row = lax.broadcasted_iota(jnp.int32, (L, cb), 0)
    half = 3 * cb

    def project(rows, cols):
        acc = jnp.dot(h_ref[0, rows, :], w_scr[:, cols], preferred_element_type=jnp.float32)
        return acc + b_ref[0, :, cols]

    for cc in range(PROJ_TM // L):
        rows = slice(cc * L, (cc + 1) * L)
        p1 = project(rows, slice(0, half))
        cu = p1[:, 2 * cb:3 * cb] * p1[:, 0:cb]
        prev = carry_scr[...]
        cu1 = jnp.where(row >= 1, pltpu.roll(cu, 1, 0), prev[7:8, :])
        cu2 = jnp.where(row >= 2, pltpu.roll(cu, 2, 0),
                        jnp.where(row == 1, prev[7:8, :], prev[6:7, :]))
        carry_scr[...] = cu[L - 8:, :]
        conv = cw_ref[0:1, :] * cu2 + cw_ref[1:2, :] * cu1 + cw_ref[2:3, :] * cu
        bconv = p1[:, cb:2 * cb] * conv

        gates = g_ref[0, rows, :]
        li = jnp.sum(jnp.where(lane == head, gates[:, :LANES], 0.0), axis=-1, keepdims=True)
        fpre = jnp.sum(jnp.where(lane == head, gates[:, LANES:], 0.0), axis=-1, keepdims=True)
        out = _mlstm_chunk(q_ref[0, rows, :], k_ref[0, rows, :], v_ref[0, rows, :],
                           o_ref[0, rows, :], z_ref[0, rows, :], li, fpre, hw_ref[...],
                           c_scr, n_scr, m_scr)
        ha_ref[0, rows, :] = out.astype(ha_ref.dtype)

        p2 = project(rows, slice(half, 2 * half))
        hb_ref[0, rows, :] = (bconv * _silu(p2[:, 0:cb])).astype(hb_ref.dtype)
        sga_ref[0, rows, :] = _sigmoid(p2[:, cb:2 * cb]).astype(sga_ref.dtype)
        sgb_ref[0, rows, :] = _sigmoid(p2[:, 2 * cb:3 * cb]).astype(sgb_ref.dtype)


def _proj_b_call(h, w_in_l, b_perm, conv_w, proj_a, gates, headnorm_w):
    bsz, s, d = h.shape
    tm = PROJ_TM
    cb = CONV_CB
    n_in = w_in_l.shape[1]
    rest_off = MLSTM_WIDTH + 2 * HEADS
    win_base = rest_off - WIN_SHIFT
    win_end = win_base + 5 * d + (HEADS - 1) * cb + WIN_W
    win_pad = max(0, win_end - n_in)

    def w_win(g):
        return pl.BlockSpec(
            (pl.Element(d), pl.Element(WIN_W, (0, win_pad))),
            lambda hd, b, m: (0, ((win_base + g * d) // LANES + hd * (cb // LANES)) * LANES),
            pipeline_mode=pl.Buffered(1))

    qk_blk = lambda off: pl.BlockSpec((1, tm, QK_DIM), lambda hd, b, m: (b, m, off + hd))
    v_blk = lambda off: pl.BlockSpec((1, tm, V_DIM), lambda hd, b, m: (b, m, off + hd))
    out_blk = pl.BlockSpec((1, tm, cb), lambda hd, b, m: (b, m, hd))
    out_sds = jax.ShapeDtypeStruct((bsz, s, d), jnp.bfloat16)
    return pl.pallas_call(
        _proj_b_kernel,
        grid=(HEADS, bsz, s // tm),
        in_specs=[
            pl.BlockSpec((1, tm, d), lambda hd, b, m: (b, m, 0)),
            w_win(0), w_win(1), w_win(2), w_win(3), w_win(4), w_win(5),
            pl.BlockSpec((1, 1, 6 * cb), lambda hd, b, m: (hd, 0, 0)),
            pl.BlockSpec((CONV_K, cb), lambda hd, b, m: (0, hd)),
            qk_blk(0), qk_blk(HEADS),
            v_blk(HEADS), v_blk(2 * HEADS), v_blk(3 * HEADS),
            pl.BlockSpec((1, tm, GATE_PAD), lambda hd, b, m: (b, m, 0)),
            pl.BlockSpec((1, V_DIM), lambda hd, b, m: (0, hd)),
        ],
        out_specs=[out_blk, out_blk, out_blk, out_blk],
        out_shape=[out_sds, out_sds, out_sds, out_sds],
        scratch_shapes=[
            pltpu.VMEM((d, 6 * cb), jnp.bfloat16),
            pltpu.VMEM((QK_DIM, V_DIM), jnp.float32),
            pltpu.VMEM((1, QK_DIM), jnp.float32),
            pltpu.VMEM((1, 1), jnp.float32),
            pltpu.VMEM((8, cb), jnp.float32),
        ],
        compiler_params=pltpu.CompilerParams(
            dimension_semantics=("arbitrary", "arbitrary", "arbitrary"),
            vmem_limit_bytes=VMEM_LIMIT),
        name="proj_conv_mlstm",
    )(h, w_in_l, w_in_l, w_in_l, w_in_l, w_in_l, w_in_l, b_perm, conv_w,
      proj_a, proj_a, proj_a, proj_a, proj_a, gates, headnorm_w)


def _merge_kernel(ha_ref, hb_ref, sga_ref, sgb_ref, x_ref, gate_ref,
                  wa_ref, wb_ref, wo_ref, nf_ref, out_ref, *, final_norm):
    y_a = jnp.dot(ha_ref[0], wa_ref[...], preferred_element_type=jnp.float32)
    y_b = jnp.dot(hb_ref[0], wb_ref[...], preferred_element_type=jnp.float32)
    merged = sga_ref[0].astype(jnp.float32) * y_a + sgb_ref[0].astype(jnp.float32) * y_b
    o = jnp.dot(merged.astype(jnp.bfloat16), wo_ref[...], preferred_element_type=jnp.float32)
    xo = x_ref[0] + gate_ref[0] * o
    if final_norm:
        ms = jnp.mean(xo * xo, axis=-1, keepdims=True)
        xo = xo * lax.rsqrt(ms + EPS) * nf_ref[...]
    out_ref[0] = xo


def _merge_call(h_a, h_b, sg_a, sg_b, x, gate, w_proj_a, w_proj_b, w_out, normf_w, final_norm):
    bsz, s, d = x.shape
    tm = MERGE_TM
    tile = pl.BlockSpec((1, tm, d), lambda b, t: (b, t, 0))
    const = lambda shape: pl.BlockSpec(shape, lambda b, t: (0,) * len(shape),
                                       pipeline_mode=pl.Buffered(1))
    return pl.pallas_call(
        functools.partial(_merge_kernel, final_norm=final_norm),
        grid=(bsz, s // tm),
        in_specs=[
            tile, tile, tile, tile,
            tile,
            pl.BlockSpec((1, 1, d), lambda b, t: (b, 0, 0)),
            const((V_WIDTH, d)), const((d, d)), const((d, d)),
            const((1, d)),
        ],
        out_specs=tile,
        out_shape=jax.ShapeDtypeStruct((bsz, s, d), jnp.float32),
        compiler_params=pltpu.CompilerParams(
            dimension_semantics=("arbitrary", "arbitrary"), vmem_limit_bytes=VMEM_LIMIT),
        name="merge_out",
    )(h_a, h_b, sg_a, sg_b, x, gate, w_proj_a, w_proj_b, w_out, normf_w)


def kernel(x, c, norm1_w, w_ada, b_ada, w_in, b_in, conv_w, headnorm_w,
           w_proj_a, w_proj_b, w_out, normf_w):
    bsz, s, d = x.shape
    depth = norm1_w.shape[0]
    gate_off = MLSTM_WIDTH
    rest_off = gate_off + 2 * HEADS
    bf16 = jnp.bfloat16

    c_pad = jnp.pad(c, ((0, 8 - bsz), (0, 0)))
    for l in range(depth):
        mod = _ada_call(c_pad, w_ada[l], b_ada[l][None, :])[:bsz]
        shift, scale, gate = (m[:, None, :] for m in jnp.split(mod, 3, axis=-1))

        w_l, b_l = w_in[l], b_in[l]
        w_a = w_l[:, :gate_off].astype(bf16)
        w_gate = jnp.zeros((d, GATE_PAD), bf16)
        w_gate = w_gate.at[:, :HEADS].set(w_l[:, gate_off:gate_off + HEADS].astype(bf16))
        w_gate = w_gate.at[:, LANES:LANES + HEADS].set(w_l[:, gate_off + HEADS:rest_off].astype(bf16))
        b_gate = jnp.zeros((GATE_PAD,), jnp.float32)
        b_gate = b_gate.at[:HEADS].set(b_l[gate_off:gate_off + HEADS])
        b_gate = b_gate.at[LANES:LANES + HEADS].set(b_l[gate_off + HEADS:rest_off])

        proj_a, gates, h = _proj_a_call(x, scale, shift, norm1_w[l][None, :], w_a,
                                        b_l[None, :gate_off], w_gate, b_gate[None, :])
        b_perm = b_l[rest_off:].reshape(6, d // CONV_CB, CONV_CB).transpose(1, 0, 2)
        b_perm = b_perm.reshape(d // CONV_CB, 1, 6 * CONV_CB)
        h_a, h_b, sg_a, sg_b = _proj_b_call(h, w_l, b_perm, conv_w[l], proj_a, gates,
                                            headnorm_w[l][None, :])
        x = _merge_call(h_a, h_b, sg_a, sg_b, x, gate, w_proj_a[l].astype(bf16),
                        w_proj_b[l].astype(bf16), w_out[l].astype(bf16), normf_w[None, :],
                        final_norm=(l == depth - 1))
    return x
```

```python
import functools
import math

import jax
import jax.numpy as jnp
from jax import lax
from jax.experimental import pallas as pl
from jax.experimental.pallas import tpu as pltpu

HEADS = 8
QK_DIM = 128
V_DIM = 256
QK_WIDTH = HEADS * QK_DIM
V_WIDTH = HEADS * V_DIM
MLSTM_WIDTH = 2 * QK_WIDTH + 3 * V_WIDTH
CONV_K = 3
EPS = 1e-6

LANES = 128
SUBLANES = 8

ADA_TN = 768
NORM_TM = 1024
PROJ_TM = 1024
PROJA_TN = 1024
MLSTM_L = 256
MERGE_TM = 256
CONV_CB = V_DIM

VMEM_LIMIT = 56 * 1024 * 1024


def _sigmoid(v):
    return 0.5 * jnp.tanh(0.5 * v) + 0.5


def _silu(v):
    return v * _sigmoid(v)


def _ada_kernel(c_ref, w_ref, b_ref, o_ref):
    c_act = _silu(c_ref[...])
    o_ref[...] = jnp.dot(c_act, w_ref[...], preferred_element_type=jnp.float32) + b_ref[...]


def _ada_call(c_pad, w_ada, b_ada):
    rows, d = c_pad.shape
    n = w_ada.shape[1]
    return pl.pallas_call(
        _ada_kernel,
        grid=(n // ADA_TN,),
        in_specs=[
            pl.BlockSpec((rows, d), lambda j: (0, 0)),
            pl.BlockSpec((d, ADA_TN), lambda j: (0, j)),
            pl.BlockSpec((1, ADA_TN), lambda j: (0, j)),
        ],
        out_specs=pl.BlockSpec((rows, ADA_TN), lambda j: (0, j)),
        out_shape=jax.ShapeDtypeStruct((rows, n), jnp.float32),
        compiler_params=pltpu.CompilerParams(
            dimension_semantics=("arbitrary",), vmem_limit_bytes=VMEM_LIMIT),
        name="adaln_mod",
    )(c_pad, w_ada, b_ada)


def _norm_kernel(x_ref, scale_ref, shift_ref, nw_ref, wg_ref, bg_ref, h_ref, g_ref):
    xf = x_ref[0]
    ms = jnp.mean(xf * xf, axis=-1, keepdims=True)
    y = xf * lax.rsqrt(ms + EPS) * nw_ref[...]
    h = (y * (1.0 + scale_ref[0]) + shift_ref[0]).astype(jnp.bfloat16)
    h_ref[0] = h
    g_ref[0] = lax.dot_general(h, wg_ref[...].astype(jnp.bfloat16), (((1,), (1,)), ((), ())),
                               preferred_element_type=jnp.float32) + bg_ref[...]


def _norm_call(x, scale, shift, norm_w, wt_gate, b_gate):
    bsz, s, d = x.shape
    tm = NORM_TM
    ng = wt_gate.shape[0]
    return pl.pallas_call(
        _norm_kernel,
        grid=(bsz, s // tm),
        in_specs=[
            pl.BlockSpec((1, tm, d), lambda b, m: (b, m, 0)),
            pl.BlockSpec((1, 1, d), lambda b, m: (b, 0, 0)),
            pl.BlockSpec((1, 1, d), lambda b, m: (b, 0, 0)),
            pl.BlockSpec((1, d), lambda b, m: (0, 0)),
            pl.BlockSpec((ng, d), lambda b, m: (0, 0)),
            pl.BlockSpec((1, ng), lambda b, m: (0, 0)),
        ],
        out_specs=[
            pl.BlockSpec((1, tm, d), lambda b, m: (b, m, 0)),
            pl.BlockSpec((1, tm, ng), lambda b, m: (b, m, 0)),
        ],
        out_shape=[
            jax.ShapeDtypeStruct((bsz, s, d), jnp.bfloat16),
            jax.ShapeDtypeStruct((bsz, s, ng), jnp.float32),
        ],
        compiler_params=pltpu.CompilerParams(
            dimension_semantics=("arbitrary", "arbitrary"), vmem_limit_bytes=VMEM_LIMIT),
        name="modnorm_gates",
    )(x, scale, shift, norm_w, wt_gate, b_gate)


def _load_weight_block(wt_ref, w_scr, col0):
    step = 256
    for r in range(0, wt_ref.shape[0], step):
        w_scr[:, col0 + r:col0 + r + step] = wt_ref[r:r + step, :].T.astype(w_scr.dtype)


def _proj_a_kernel(h_ref, wt_ref, b_ref, o_ref, w_scr):
    @pl.when((pl.program_id(1) == 0) & (pl.program_id(2) == 0))
    def _():
        _load_weight_block(wt_ref, w_scr, 0)

    acc = jnp.dot(h_ref[0], w_scr[...], preferred_element_type=jnp.float32)
    o_ref[0] = (acc + b_ref[...]).astype(o_ref.dtype)


def _proj_a_call(h, wt, b_a):
    bsz, s, d = h.shape
    n = MLSTM_WIDTH
    tm, tn = PROJ_TM, PROJA_TN
    return pl.pallas_call(
        _proj_a_kernel,
        grid=(n // tn, bsz, s // tm),
        in_specs=[
            pl.BlockSpec((1, tm, d), lambda j, b, m: (b, m, 0)),
            pl.BlockSpec((tn, d), lambda j, b, m: (j, 0)),
            pl.BlockSpec((1, tn), lambda j, b, m: (0, j)),
        ],
        out_specs=pl.BlockSpec((1, tm, tn), lambda j, b, m: (b, m, j)),
        out_shape=jax.ShapeDtypeStruct((bsz, s, n), jnp.bfloat16),
        scratch_shapes=[pltpu.VMEM((d, tn), jnp.bfloat16)],
        compiler_params=pltpu.CompilerParams(
            dimension_semantics=("arbitrary", "arbitrary", "arbitrary"),
            vmem_limit_bytes=VMEM_LIMIT),
        name="proj_mlstm",
    )(h, wt, b_a)


def _prefix_scan_rows(v, combine, fill):
    n = v.shape[0]
    row = lax.broadcasted_iota(jnp.int32, v.shape, 0)
    step = 1
    while step < n:
        shifted = pltpu.roll(v, step, 0)
        v = combine(v, jnp.where(row >= step, shifted, fill))
        step *= 2
    return v


def _mlstm_chunk(qh, kh, vh, og, zg, li, fpre, hw, c_scr, n_scr, m_scr):
    L = qh.shape[0]
    lf = jax.nn.log_sigmoid(fpre)
    b = _prefix_scan_rows(lf, jnp.add, 0.0)
    a = li - b
    cm = _prefix_scan_rows(a, jnp.maximum, -jnp.inf)
    m_state = m_scr[...]
    big_m = jnp.maximum(m_state, cm)
    w_inter = jnp.exp(m_state - big_m) * (QK_DIM ** -0.5)
    clamp = jnp.exp(-b - big_m)
    m_last = big_m[L - 1:L, :]
    ws = jnp.exp(a - m_last)
    keep = jnp.exp(m_state - m_last)
    m_scr[...] = b[L - 1:L, :] + m_last
    a_row = jnp.broadcast_to(a, (L, LANES)).T[0:1, :]
    neg_m = math.log(QK_DIM ** -0.5) - big_m

    ri = lax.broadcasted_iota(jnp.int32, (L, L), 0)
    ci = lax.broadcasted_iota(jnp.int32, (L, L), 1)
    c_h = c_scr[...]
    n_h = n_scr[...]

    qk = lax.dot_general(qh, kh, (((1,), (1,)), ((), ())), preferred_element_type=jnp.float32)
    scores = jnp.where(ci <= ri, jnp.exp(a_row + neg_m), 0.0) * qk
    num = jnp.dot(scores.astype(jnp.bfloat16), vh, preferred_element_type=jnp.float32)
    num = num + w_inter * jnp.dot(qh, c_h.astype(jnp.bfloat16), preferred_element_type=jnp.float32)
    den = jnp.sum(scores, axis=-1, keepdims=True) \
        + w_inter * jnp.sum(qh.astype(jnp.float32) * n_h, axis=-1, keepdims=True)
    h_out = num * (1.0 / jnp.maximum(jnp.abs(den), clamp))

    vw = (ws * vh.astype(jnp.float32)).astype(jnp.bfloat16)
    kv = lax.dot_general(kh, vw, (((0,), (0,)), ((), ())), preferred_element_type=jnp.float32)
    c_scr[...] = keep * c_h + kv
    n_scr[...] = keep * n_h + jnp.sum(ws * kh.astype(jnp.float32), axis=0, keepdims=True)

    hg = _sigmoid(og.astype(jnp.float32)) * h_out
    ms = jnp.mean(hg * hg, axis=-1, keepdims=True)
    return hg * lax.rsqrt(ms + EPS) * hw * _silu(zg.astype(jnp.float32))


def _proj_b_kernel(h_ref, wu_ref, wbg_ref, wcg_ref, wzb_ref, wga_ref, wgb_ref, b_ref, cw_ref,
                   q_ref, k_ref, v_ref, o_ref, z_ref, g_ref, hw_ref,
                   ha_ref, hb_ref, sga_ref, sgb_ref,
                   w_scr, c_scr, n_scr, m_scr, carry_scr):
    head = pl.program_id(0)
    cb = CONV_CB

    @pl.when((pl.program_id(1) == 0) & (pl.program_id(2) == 0))
    def _():
        for g, win in enumerate((wu_ref, wbg_ref, wcg_ref, wzb_ref, wga_ref, wgb_ref)):
            _load_weight_block(win, w_scr, g * cb)

    @pl.when(pl.program_id(2) == 0)
    def _():
        c_scr[...] = jnp.zeros_like(c_scr)
        n_scr[...] = jnp.zeros_like(n_scr)
        m_scr[...] = jnp.zeros_like(m_scr)
        carry_scr[...] = jnp.zeros_like(carry_scr)

    L = MLSTM_L
    lane = lax.broadcasted_iota(jnp.int32, (L, 2 * HEADS), 1)
    row = lax.broadcasted_iota(jnp.int32, (L, cb), 0)
    half = 3 * cb

    def project(rows, cols):
        acc = jnp.dot(h_ref[0, rows, :], w_scr[:, cols], preferred_element_type=jnp.float32)
        return acc + b_ref[0, :, cols]

    for cc in range(PROJ_TM // L):
        rows = slice(cc * L, (cc + 1) * L)
        p1 = project(rows, slice(0, half))
        cu = p1[:, 2 * cb:3 * cb] * p1[:, 0:cb]
        prev = carry_scr[...]
        cu1 = jnp.where(row >= 1, pltpu.roll(cu, 1, 0), prev[7:8, :])
        cu2 = jnp.where(row >= 2, pltpu.roll(cu, 2, 0),
                        jnp.where(row == 1, prev[7:8, :], prev[6:7, :]))
        carry_scr[...] = cu[L - 8:, :]
        conv = cw_ref[0:1, :] * cu2 + cw_ref[1:2, :] * cu1 + cw_ref[2:3, :] * cu
        bconv = p1[:, cb:2 * cb] * conv

        gates = g_ref[0, rows, :]
        li = jnp.sum(jnp.where(lane == head, gates, 0.0), axis=-1, keepdims=True)
        fpre = jnp.sum(jnp.where(lane == head + HEADS, gates, 0.0), axis=-1, keepdims=True)
        out = _mlstm_chunk(q_ref[0, rows, :], k_ref[0, rows, :], v_ref[0, rows, :],
                           o_ref[0, rows, :], z_ref[0, rows, :], li, fpre, hw_ref[...],
                           c_scr, n_scr, m_scr)
        ha_ref[0, rows, :] = out.astype(ha_ref.dtype)

        p2 = project(rows, slice(half, 2 * half))
        hb_ref[0, rows, :] = (bconv * _silu(p2[:, 0:cb])).astype(hb_ref.dtype)
        sga_ref[0, rows, :] = _sigmoid(p2[:, cb:2 * cb]).astype(sga_ref.dtype)
        sgb_ref[0, rows, :] = _sigmoid(p2[:, 2 * cb:3 * cb]).astype(sgb_ref.dtype)


def _proj_b_call(h, wt, b_perm, conv_w, proj_a, gates, headnorm_w):
    bsz, s, d = h.shape
    tm = PROJ_TM
    cb = CONV_CB
    rest_off = MLSTM_WIDTH + 2 * HEADS
    assert rest_off % SUBLANES == 0 and cb % SUBLANES == 0 and d % SUBLANES == 0

    def w_win(g):
        return pl.BlockSpec(
            (pl.Element(cb), pl.Element(d)),
            lambda hd, b, m: (((rest_off + g * d) // SUBLANES + hd * (cb // SUBLANES)) * SUBLANES, 0))

    qk_blk = lambda off: pl.BlockSpec((1, tm, QK_DIM), lambda hd, b, m: (b, m, off + hd))
    v_blk = lambda off: pl.BlockSpec((1, tm, V_DIM), lambda hd, b, m: (b, m, off + hd))
    out_blk = pl.BlockSpec((1, tm, cb), lambda hd, b, m: (b, m, hd))
    out_sds = jax.ShapeDtypeStruct((bsz, s, d), jnp.bfloat16)
    return pl.pallas_call(
        _proj_b_kernel,
        grid=(HEADS, bsz, s // tm),
        in_specs=[
            pl.BlockSpec((1, tm, d), lambda hd, b, m: (b, m, 0)),
            w_win(0), w_win(1), w_win(2), w_win(3), w_win(4), w_win(5),
            pl.BlockSpec((1, 1, 6 * cb), lambda hd, b, m: (hd, 0, 0)),
            pl.BlockSpec((CONV_K, cb), lambda hd, b, m: (0, hd)),
            qk_blk(0), qk_blk(HEADS),
            v_blk(HEADS), v_blk(2 * HEADS), v_blk(3 * HEADS),
            pl.BlockSpec((1, tm, 2 * HEADS), lambda hd, b, m: (b, m, 0)),
            pl.BlockSpec((1, V_DIM), lambda hd, b, m: (0, hd)),
        ],
        out_specs=[out_blk, out_blk, out_blk, out_blk],
        out_shape=[out_sds, out_sds, out_sds, out_sds],
        scratch_shapes=[
            pltpu.VMEM((d, 6 * cb), jnp.bfloat16),
            pltpu.VMEM((QK_DIM, V_DIM), jnp.float32),
            pltpu.VMEM((1, QK_DIM), jnp.float32),
            pltpu.VMEM((1, 1), jnp.float32),
            pltpu.VMEM((8, cb), jnp.float32),
        ],
        compiler_params=pltpu.CompilerParams(
            dimension_semantics=("arbitrary", "arbitrary", "arbitrary"),
            vmem_limit_bytes=VMEM_LIMIT),
        name="proj_conv_mlstm",
    )(h, wt, wt, wt, wt, wt, wt, b_perm, conv_w,
      proj_a, proj_a, proj_a, proj_a, proj_a, gates, headnorm_w)


def _merge_kernel(ha_ref, hb_ref, sga_ref, sgb_ref, x_ref, gate_ref,
                  wa_ref, wb_ref, wo_ref, nf_ref, out_ref, *, final_norm):
    y_a = jnp.dot(ha_ref[0], wa_ref[...], preferred_element_type=jnp.float32)
    y_b = jnp.dot(hb_ref[0], wb_ref[...], preferred_element_type=jnp.float32)
    merged = sga_ref[0].astype(jnp.float32) * y_a + sgb_ref[0].astype(jnp.float32) * y_b
    o = jnp.dot(merged.astype(jnp.bfloat16), wo_ref[...], preferred_element_type=jnp.float32)
    xo = x_ref[0] + gate_ref[0] * o
    if final_norm:
        ms = jnp.mean(xo * xo, axis=-1, keepdims=True)
        xo = xo * lax.rsqrt(ms + EPS) * nf_ref[...]
    out_ref[0] = xo


def _merge_call(h_a, h_b, sg_a, sg_b, x, gate, w_proj_a, w_proj_b, w_out, normf_w, final_norm):
    bsz, s, d = x.shape
    tm = MERGE_TM
    tile = pl.BlockSpec((1, tm, d), lambda b, t: (b, t, 0))
    const = lambda shape: pl.BlockSpec(shape, lambda b, t: (0,) * len(shape),
                                       pipeline_mode=pl.Buffered(1))
    return pl.pallas_call(
        functools.partial(_merge_kernel, final_norm=final_norm),
        grid=(bsz, s // tm),
        in_specs=[
            tile, tile, tile, tile,
            tile,
            pl.BlockSpec((1, 1, d), lambda b, t: (b, 0, 0)),
            const((V_WIDTH, d)), const((d, d)), const((d, d)),
            const((1, d)),
        ],
        out_specs=tile,
        out_shape=jax.ShapeDtypeStruct((bsz, s, d), jnp.float32),
        compiler_params=pltpu.CompilerParams(
            dimension_semantics=("arbitrary", "arbitrary"), vmem_limit_bytes=VMEM_LIMIT),
        name="merge_out",
    )(h_a, h_b, sg_a, sg_b, x, gate, w_proj_a, w_proj_b, w_out, normf_w)


def kernel(x, c, norm1_w, w_ada, b_ada, w_in, b_in, conv_w, headnorm_w,
           w_proj_a, w_proj_b, w_out, normf_w):
    bsz, s, d = x.shape
    depth = norm1_w.shape[0]
    gate_off = MLSTM_WIDTH
    rest_off = gate_off + 2 * HEADS
    bf16 = jnp.bfloat16

    c_pad = jnp.pad(c, ((0, 8 - bsz), (0, 0)))
    for l in range(depth):
        mod = _ada_call(c_pad, w_ada[l], b_ada[l][None, :])[:bsz]
        shift, scale, gate = (m[:, None, :] for m in jnp.split(mod, 3, axis=-1))

        wt, b_l = jnp.swapaxes(w_in[l], 0, 1), b_in[l]
        h, gates = _norm_call(x, scale, shift, norm1_w[l][None, :], wt[gate_off:rest_off],
                              b_l[None, gate_off:rest_off])
        proj_a = _proj_a_call(h, wt, b_l[None, :gate_off])
        b_perm = b_l[rest_off:].reshape(6, d // CONV_CB, CONV_CB).transpose(1, 0, 2)
        b_perm = b_perm.reshape(d // CONV_CB, 1, 6 * CONV_CB)
        h_a, h_b, sg_a, sg_b = _proj_b_call(h, wt, b_perm, conv_w[l], proj_a, gates,
                                            headnorm_w[l][None, :])
        x = _merge_call(h_a, h_b, sg_a, sg_b, x, gate, w_proj_a[l].astype(bf16),
                        w_proj_b[l].astype(bf16), w_out[l].astype(bf16), normf_w[None, :],
                        final_norm=(l == depth - 1))
    return x
```

```python
import functools
import math

import jax
import jax.numpy as jnp
from jax import lax
from jax.experimental import pallas as pl
from jax.experimental.pallas import tpu as pltpu

HEADS = 8
QK_DIM = 128
V_DIM = 256
QK_WIDTH = HEADS * QK_DIM
V_WIDTH = HEADS * V_DIM
MLSTM_WIDTH = 2 * QK_WIDTH + 3 * V_WIDTH
CONV_K = 3
EPS = 1e-6

LANES = 128
SUBLANES = 8

ADA_TN = 768
NORM_TM = 1024
PROJ_TM = 1024
PROJA_TM = 2048
PROJA_TN = 1024
MLSTM_L = 256
MERGE_TM = 256
CONV_CB = V_DIM

VMEM_LIMIT = 56 * 1024 * 1024


def _sigmoid(v):
    return 0.5 * jnp.tanh(0.5 * v) + 0.5


def _silu(v):
    return v * _sigmoid(v)


def _ada_kernel(c_ref, w_ref, b_ref, o_ref):
    c_act = _silu(c_ref[...])
    o_ref[...] = jnp.dot(c_act, w_ref[...], preferred_element_type=jnp.float32) + b_ref[...]


def _ada_call(c_pad, w_ada, b_ada):
    rows, d = c_pad.shape
    n = w_ada.shape[1]
    return pl.pallas_call(
        _ada_kernel,
        grid=(n // ADA_TN,),
        in_specs=[
            pl.BlockSpec((rows, d), lambda j: (0, 0)),
            pl.BlockSpec((d, ADA_TN), lambda j: (0, j)),
            pl.BlockSpec((1, ADA_TN), lambda j: (0, j)),
        ],
        out_specs=pl.BlockSpec((rows, ADA_TN), lambda j: (0, j)),
        out_shape=jax.ShapeDtypeStruct((rows, n), jnp.float32),
        compiler_params=pltpu.CompilerParams(
            dimension_semantics=("arbitrary",), vmem_limit_bytes=VMEM_LIMIT),
        name="adaln_mod",
    )(c_pad, w_ada, b_ada)


def _chunk_scan_rows(v, combine, fill, chunk):
    pos = lax.broadcasted_iota(jnp.int32, v.shape, 0) % chunk
    step = 1
    while step < chunk:
        shifted = pltpu.roll(v, step, 0)
        v = combine(v, jnp.where(pos >= step, shifted, fill))
        step *= 2
    return v


def _norm_kernel(x_ref, scale_ref, shift_ref, nw_ref, wg_ref, bg_ref,
                 h_ref, b_ref, a_ref, cm_ref, at_ref):
    xf = x_ref[0]
    ms = jnp.mean(xf * xf, axis=-1, keepdims=True)
    y = xf * lax.rsqrt(ms + EPS) * nw_ref[...]
    h = (y * (1.0 + scale_ref[0]) + shift_ref[0]).astype(jnp.bfloat16)
    h_ref[0] = h
    g = lax.dot_general(h, wg_ref[...].astype(jnp.bfloat16), (((1,), (1,)), ((), ())),
                        preferred_element_type=jnp.float32) + bg_ref[...]
    lf = jax.nn.log_sigmoid(pltpu.roll(g, LANES - HEADS, 1))
    b = _chunk_scan_rows(lf, jnp.add, 0.0, MLSTM_L)
    a = g - b
    cm = _chunk_scan_rows(a, jnp.maximum, -jnp.inf, MLSTM_L)
    b_ref[0] = b[:, :HEADS]
    a_ref[0] = a[:, :HEADS]
    cm_ref[0] = cm[:, :HEADS]
    at_ref[0] = a.T[:HEADS, :]


def _norm_call(x, scale, shift, norm_w, wt_gate, b_gate):
    bsz, s, d = x.shape
    tm = NORM_TM
    col_blk = pl.BlockSpec((1, tm, HEADS), lambda b, m: (b, m, 0))
    col_sds = jax.ShapeDtypeStruct((bsz, s, HEADS), jnp.float32)
    return pl.pallas_call(
        _norm_kernel,
        grid=(bsz, s // tm),
        in_specs=[
            pl.BlockSpec((1, tm, d), lambda b, m: (b, m, 0)),
            pl.BlockSpec((1, 1, d), lambda b, m: (b, 0, 0)),
            pl.BlockSpec((1, 1, d), lambda b, m: (b, 0, 0)),
            pl.BlockSpec((1, d), lambda b, m: (0, 0)),
            pl.BlockSpec((LANES, d), lambda b, m: (0, 0)),
            pl.BlockSpec((1, LANES), lambda b, m: (0, 0)),
        ],
        out_specs=[
            pl.BlockSpec((1, tm, d), lambda b, m: (b, m, 0)),
            col_blk, col_blk, col_blk,
            pl.BlockSpec((1, HEADS, tm), lambda b, m: (b, 0, m)),
        ],
        out_shape=[
            jax.ShapeDtypeStruct((bsz, s, d), jnp.bfloat16),
            col_sds, col_sds, col_sds,
            jax.ShapeDtypeStruct((bsz, HEADS, s), jnp.float32),
        ],
        compiler_params=pltpu.CompilerParams(
            dimension_semantics=("arbitrary", "arbitrary"), vmem_limit_bytes=VMEM_LIMIT),
        name="modnorm_gates",
    )(x, scale, shift, norm_w, wt_gate, b_gate)


def _load_weight_block(wt_ref, w_scr, col0):
    step = 256
    for r in range(0, wt_ref.shape[0], step):
        w_scr[:, col0 + r:col0 + r + step] = wt_ref[r:r + step, :].T.astype(w_scr.dtype)


def _proj_a_kernel(h_ref, wt_ref, b_ref, o_ref, w_scr):
    @pl.when((pl.program_id(1) == 0) & (pl.program_id(2) == 0))
    def _():
        _load_weight_block(wt_ref, w_scr, 0)

    acc = jnp.dot(h_ref[0], w_scr[...], preferred_element_type=jnp.float32)
    o_ref[0] = (acc + b_ref[...]).astype(o_ref.dtype)


def _proj_a_call(h, wt, b_a):
    bsz, s, d = h.shape
    n = MLSTM_WIDTH
    tm, tn = PROJA_TM, PROJA_TN
    return pl.pallas_call(
        _proj_a_kernel,
        grid=(n // tn, bsz, s // tm),
        in_specs=[
            pl.BlockSpec((1, tm, d), lambda j, b, m: (b, m, 0)),
            pl.BlockSpec((tn, d), lambda j, b, m: (j, 0)),
            pl.BlockSpec((1, tn), lambda j, b, m: (0, j)),
        ],
        out_specs=pl.BlockSpec((1, tm, tn), lambda j, b, m: (b, m, j)),
        out_shape=jax.ShapeDtypeStruct((bsz, s, n), jnp.bfloat16),
        scratch_shapes=[pltpu.VMEM((d, tn), jnp.bfloat16)],
        compiler_params=pltpu.CompilerParams(
            dimension_semantics=("arbitrary", "arbitrary", "arbitrary"),
            vmem_limit_bytes=VMEM_LIMIT),
        name="proj_mlstm",
    )(h, wt, b_a)


def _mlstm_chunk(qh, kh, vh, og, zg, b, a, cm, a_row, hw, c_scr, n_scr, m_scr):
    L = qh.shape[0]
    m_state = m_scr[...]
    big_m = jnp.maximum(m_state, cm)
    w_inter = jnp.exp(m_state - big_m) * (QK_DIM ** -0.5)
    clamp = jnp.exp(-b - big_m)
    m_last = big_m[L - 1:L, :]
    ws = jnp.exp(a - m_last)
    keep = jnp.exp(m_state - m_last)
    m_scr[...] = b[L - 1:L, :] + m_last
    neg_m = math.log(QK_DIM ** -0.5) - big_m

    ri = lax.broadcasted_iota(jnp.int32, (L, L), 0)
    ci = lax.broadcasted_iota(jnp.int32, (L, L), 1)
    c_h = c_scr[...]
    n_h = n_scr[...]

    qk = lax.dot_general(qh, kh, (((1,), (1,)), ((), ())), preferred_element_type=jnp.float32)
    scores = jnp.where(ci <= ri, jnp.exp(a_row + neg_m), 0.0) * qk
    num = jnp.dot(scores.astype(jnp.bfloat16), vh, preferred_element_type=jnp.float32)
    num = num + w_inter * jnp.dot(qh, c_h.astype(jnp.bfloat16), preferred_element_type=jnp.float32)
    den = jnp.sum(scores, axis=-1, keepdims=True) \
        + w_inter * jnp.sum(qh.astype(jnp.float32) * n_h, axis=-1, keepdims=True)
    h_out = num * (1.0 / jnp.maximum(jnp.abs(den), clamp))

    vw = (ws * vh.astype(jnp.float32)).astype(jnp.bfloat16)
    kv = lax.dot_general(kh, vw, (((0,), (0,)), ((), ())), preferred_element_type=jnp.float32)
    c_scr[...] = keep * c_h + kv
    n_scr[...] = keep * n_h + jnp.sum(ws * kh.astype(jnp.float32), axis=0, keepdims=True)

    hg = _sigmoid(og.astype(jnp.float32)) * h_out
    ms = jnp.mean(hg * hg, axis=-1, keepdims=True)
    return hg * lax.rsqrt(ms + EPS) * hw * _silu(zg.astype(jnp.float32))


def _proj_b_kernel(h_ref, wu_ref, wbg_ref, wcg_ref, wzb_ref, wga_ref, wgb_ref, b_ref, cw_ref,
                   q_ref, k_ref, v_ref, o_ref, z_ref, gb_ref, ga_ref, gcm_ref, gat_ref, hw_ref,
                   ha_ref, hb_ref, sga_ref, sgb_ref,
                   w_scr, c_scr, n_scr, m_scr, carry_scr):
    head = pl.program_id(0)
    cb = CONV_CB

    @pl.when((pl.program_id(1) == 0) & (pl.program_id(2) == 0))
    def _():
        for g, win in enumerate((wu_ref, wbg_ref, wcg_ref, wzb_ref, wga_ref, wgb_ref)):
            _load_weight_block(win, w_scr, g * cb)

    @pl.when(pl.program_id(2) == 0)
    def _():
        c_scr[...] = jnp.zeros_like(c_scr)
        n_scr[...] = jnp.zeros_like(n_scr)
        m_scr[...] = jnp.zeros_like(m_scr)
        carry_scr[...] = jnp.zeros_like(carry_scr)

    L = MLSTM_L
    lane = lax.broadcasted_iota(jnp.int32, (L, HEADS), 1)
    row = lax.broadcasted_iota(jnp.int32, (L, cb), 0)

    def head_col(ref, rows):
        return jnp.sum(jnp.where(lane == head, ref[0, rows, :], 0.0), axis=-1, keepdims=True)

    def project(rows, half):
        cols = slice(half * 3 * cb, (half + 1) * 3 * cb)
        acc = jnp.dot(h_ref[0, rows, :], w_scr[:, cols], preferred_element_type=jnp.float32)
        return acc + b_ref[0, :, cols]

    for cc in range(h_ref.shape[1] // L):
        rows = slice(cc * L, (cc + 1) * L)
        p1 = project(rows, 0)
        cu = p1[:, 2 * cb:3 * cb] * p1[:, 0:cb]
        prev = carry_scr[...]
        cu1 = jnp.where(row >= 1, pltpu.roll(cu, 1, 0), prev[7:8, :])
        cu2 = jnp.where(row >= 2, pltpu.roll(cu, 2, 0),
                        jnp.where(row == 1, prev[7:8, :], prev[6:7, :]))
        carry_scr[...] = cu[L - 8:, :]
        conv = cw_ref[0:1, :] * cu2 + cw_ref[1:2, :] * cu1 + cw_ref[2:3, :] * cu
        bconv = p1[:, cb:2 * cb] * conv

        out = _mlstm_chunk(q_ref[0, rows, :], k_ref[0, rows, :], v_ref[0, rows, :],
                           o_ref[0, rows, :], z_ref[0, rows, :],
                           head_col(gb_ref, rows), head_col(ga_ref, rows), head_col(gcm_ref, rows),
                           gat_ref[0, pl.ds(head, 1), rows], hw_ref[...], c_scr, n_scr, m_scr)
        ha_ref[0, rows, :] = out.astype(ha_ref.dtype)

        p2 = project(rows, 1)
        hb_ref[0, rows, :] = (bconv * _silu(p2[:, 0:cb])).astype(hb_ref.dtype)
        sga_ref[0, rows, :] = _sigmoid(p2[:, cb:2 * cb]).astype(sga_ref.dtype)
        sgb_ref[0, rows, :] = _sigmoid(p2[:, 2 * cb:3 * cb]).astype(sgb_ref.dtype)


def _proj_b_call(h, wt, b_perm, conv_w, proj_a, g_b, g_a, g_cm, g_at, headnorm_w):
    bsz, s, d = h.shape
    tm = PROJ_TM
    cb = CONV_CB
    rest_off = MLSTM_WIDTH + 2 * HEADS
    assert rest_off % SUBLANES == 0 and cb % SUBLANES == 0 and d % SUBLANES == 0

    def w_win(g):
        return pl.BlockSpec(
            (pl.Element(cb), pl.Element(d)),
            lambda hd, b, m: (((rest_off + g * d) // SUBLANES + hd * (cb // SUBLANES)) * SUBLANES, 0))

    qk_blk = lambda off: pl.BlockSpec((1, tm, QK_DIM), lambda hd, b, m: (b, m, off + hd))
    v_blk = lambda off: pl.BlockSpec((1, tm, V_DIM), lambda hd, b, m: (b, m, off + hd))
    gate_blk = pl.BlockSpec((1, tm, HEADS), lambda hd, b, m: (b, m, 0))
    out_blk = pl.BlockSpec((1, tm, cb), lambda hd, b, m: (b, m, hd))
    out_sds = jax.ShapeDtypeStruct((bsz, s, d), jnp.bfloat16)
    return pl.pallas_call(
        _proj_b_kernel,
        grid=(HEADS, bsz, s // tm),
        in_specs=[
            pl.BlockSpec((1, tm, d), lambda hd, b, m: (b, m, 0)),
            w_win(0), w_win(1), w_win(2), w_win(3), w_win(4), w_win(5),
            pl.BlockSpec((1, 1, 6 * cb), lambda hd, b, m: (hd, 0, 0)),
            pl.BlockSpec((CONV_K, cb), lambda hd, b, m: (0, hd)),
            qk_blk(0), qk_blk(HEADS),
            v_blk(HEADS), v_blk(2 * HEADS), v_blk(3 * HEADS),
            gate_blk, gate_blk, gate_blk,
            pl.BlockSpec((1, HEADS, tm), lambda hd, b, m: (b, 0, m)),
            pl.BlockSpec((1, V_DIM), lambda hd, b, m: (0, hd)),
        ],
        out_specs=[out_blk, out_blk, out_blk, out_blk],
        out_shape=[out_sds, out_sds, out_sds, out_sds],
        scratch_shapes=[
            pltpu.VMEM((d, 6 * cb), jnp.bfloat16),
            pltpu.VMEM((QK_DIM, V_DIM), jnp.float32),
            pltpu.VMEM((1, QK_DIM), jnp.float32),
            pltpu.VMEM((1, 1), jnp.float32),
            pltpu.VMEM((8, cb), jnp.float32),
        ],
        compiler_params=pltpu.CompilerParams(
            dimension_semantics=("arbitrary", "arbitrary", "arbitrary"),
            vmem_limit_bytes=VMEM_LIMIT),
        name="proj_conv_mlstm",
    )(h, wt, wt, wt, wt, wt, wt, b_perm, conv_w,
      proj_a, proj_a, proj_a, proj_a, proj_a, g_b, g_a, g_cm, g_at, headnorm_w)


def _merge_kernel(ha_ref, hb_ref, sga_ref, sgb_ref, x_ref, gate_ref,
                  wa_ref, wb_ref, wo_ref, nf_ref, out_ref, *, final_norm):
    y_a = jnp.dot(ha_ref[0], wa_ref[...], preferred_element_type=jnp.float32)
    y_b = jnp.dot(hb_ref[0], wb_ref[...], preferred_element_type=jnp.float32)
    merged = sga_ref[0].astype(jnp.float32) * y_a + sgb_ref[0].astype(jnp.float32) * y_b
    o = jnp.dot(merged.astype(jnp.bfloat16), wo_ref[...], preferred_element_type=jnp.float32)
    xo = x_ref[0] + gate_ref[0] * o
    if final_norm:
        ms = jnp.mean(xo * xo, axis=-1, keepdims=True)
        xo = xo * lax.rsqrt(ms + EPS) * nf_ref[...]
    out_ref[0] = xo


def _merge_call(h_a, h_b, sg_a, sg_b, x, gate, w_proj_a, w_proj_b, w_out, normf_w, final_norm):
    bsz, s, d = x.shape
    tm = MERGE_TM
    tile = pl.BlockSpec((1, tm, d), lambda b, t: (b, t, 0))
    const = lambda shape: pl.BlockSpec(shape, lambda b, t: (0,) * len(shape),
                                       pipeline_mode=pl.Buffered(1))
    return pl.pallas_call(
        functools.partial(_merge_kernel, final_norm=final_norm),
        grid=(bsz, s // tm),
        in_specs=[
            tile, tile, tile, tile,
            tile,
            pl.BlockSpec((1, 1, d), lambda b, t: (b, 0, 0)),
            const((V_WIDTH, d)), const((d, d)), const((d, d)),
            const((1, d)),
        ],
        out_specs=tile,
        out_shape=jax.ShapeDtypeStruct((bsz, s, d), jnp.float32),
        compiler_params=pltpu.CompilerParams(
            dimension_semantics=("arbitrary", "arbitrary"), vmem_limit_bytes=VMEM_LIMIT),
        name="merge_out",
    )(h_a, h_b, sg_a, sg_b, x, gate, w_proj_a, w_proj_b, w_out, normf_w)


def kernel(x, c, norm1_w, w_ada, b_ada, w_in, b_in, conv_w, headnorm_w,
           w_proj_a, w_proj_b, w_out, normf_w):
    bsz, s, d = x.shape
    depth = norm1_w.shape[0]
    gate_off = MLSTM_WIDTH
    rest_off = gate_off + 2 * HEADS
    bf16 = jnp.bfloat16

    c_pad = jnp.pad(c, ((0, 8 - bsz), (0, 0)))
    for l in range(depth):
        mod = _ada_call(c_pad, w_ada[l], b_ada[l][None, :])[:bsz]
        shift, scale, gate = (m[:, None, :] for m in jnp.split(mod, 3, axis=-1))

        wt, b_l = jnp.swapaxes(w_in[l], 0, 1), b_in[l]
        gate_pad = LANES - 2 * HEADS
        h, g_b, g_a, g_cm, g_at = _norm_call(
            x, scale, shift, norm1_w[l][None, :],
            jnp.pad(wt[gate_off:rest_off], ((0, gate_pad), (0, 0))),
            jnp.pad(b_l[None, gate_off:rest_off], ((0, 0), (0, gate_pad))))
        proj_a = _proj_a_call(h, wt, b_l[None, :gate_off])
        b_perm = b_l[rest_off:].reshape(6, d // CONV_CB, CONV_CB).transpose(1, 0, 2)
        b_perm = b_perm.reshape(d // CONV_CB, 1, 6 * CONV_CB)
        h_a, h_b, sg_a, sg_b = _proj_b_call(h, wt, b_perm, conv_w[l], proj_a, g_b, g_a, g_cm, g_at,
                                            headnorm_w[l][None, :])
        x = _merge_call(h_a, h_b, sg_a, sg_b, x, gate, w_proj_a[l].astype(bf16),
                        w_proj_b[l].astype(bf16), w_out[l].astype(bf16), normf_w[None, :],
                        final_norm=(l == depth - 1))
    return x
```

```python
import functools
import math

import jax
import jax.numpy as jnp
from jax import lax
from jax.experimental import pallas as pl
from jax.experimental.pallas import tpu as pltpu

HEADS = 8
QK_DIM = 128
V_DIM = 256
QK_WIDTH = HEADS * QK_DIM
V_WIDTH = HEADS * V_DIM
MLSTM_WIDTH = 2 * QK_WIDTH + 3 * V_WIDTH
CONV_K = 3
EPS = 1e-6

LANES = 128
SUBLANES = 8

ADA_TN = 768
NORM_TM = 1024
PROJ_TM = 1024
PROJA_TM = 2048
PROJA_TN = 1024
MLSTM_L = 256
MERGE_TM = 256
CONV_CB = V_DIM

VMEM_LIMIT = 58 * 1024 * 1024


def _sigmoid(v):
    return 0.5 * jnp.tanh(0.5 * v) + 0.5


def _silu(v):
    return v * _sigmoid(v)


def _ada_kernel(c_ref, w_ref, b_ref, o_ref):
    c_act = _silu(c_ref[...])
    o_ref[...] = jnp.dot(c_act, w_ref[...], preferred_element_type=jnp.float32) + b_ref[...]


def _ada_call(c_pad, w_ada, b_ada):
    rows, d = c_pad.shape
    n = w_ada.shape[1]
    return pl.pallas_call(
        _ada_kernel,
        grid=(n // ADA_TN,),
        in_specs=[
            pl.BlockSpec((rows, d), lambda j: (0, 0)),
            pl.BlockSpec((d, ADA_TN), lambda j: (0, j)),
            pl.BlockSpec((1, ADA_TN), lambda j: (0, j)),
        ],
        out_specs=pl.BlockSpec((rows, ADA_TN), lambda j: (0, j)),
        out_shape=jax.ShapeDtypeStruct((rows, n), jnp.float32),
        compiler_params=pltpu.CompilerParams(
            dimension_semantics=("arbitrary",), vmem_limit_bytes=VMEM_LIMIT),
        name="adaln_mod",
    )(c_pad, w_ada, b_ada)


def _chunk_scan_rows(v, combine, fill, chunk):
    pos = lax.broadcasted_iota(jnp.int32, v.shape, 0) % chunk
    step = 1
    while step < chunk:
        shifted = pltpu.roll(v, step, 0)
        v = combine(v, jnp.where(pos >= step, shifted, fill))
        step *= 2
    return v


def _norm_kernel(x_ref, scale_ref, shift_ref, nw_ref, wg_ref, bg_ref, h_ref, cols_ref, at_ref):
    xf = x_ref[0]
    ms = jnp.mean(xf * xf, axis=-1, keepdims=True)
    y = xf * lax.rsqrt(ms + EPS) * nw_ref[...]
    h = (y * (1.0 + scale_ref[0]) + shift_ref[0]).astype(jnp.bfloat16)
    h_ref[0] = h
    g = lax.dot_general(h, wg_ref[...].astype(jnp.bfloat16), (((1,), (1,)), ((), ())),
                        preferred_element_type=jnp.float32) + bg_ref[...]
    lf = jax.nn.log_sigmoid(pltpu.roll(g, LANES - HEADS, 1))
    b = _chunk_scan_rows(lf, jnp.add, 0.0, MLSTM_L)
    a = g - b
    cm = _chunk_scan_rows(a, jnp.maximum, -jnp.inf, MLSTM_L)
    lane = lax.broadcasted_iota(jnp.int32, g.shape, 1)
    cols_ref[0] = jnp.where(lane < HEADS, b,
                            jnp.where(lane < 2 * HEADS, pltpu.roll(a, HEADS, 1),
                                      pltpu.roll(cm, 2 * HEADS, 1)))
    at_ref[0] = a.T[:HEADS, :]


def _norm_call(x, scale, shift, norm_w, wt_gate, b_gate):
    bsz, s, d = x.shape
    tm = NORM_TM
    return pl.pallas_call(
        _norm_kernel,
        grid=(bsz, s // tm),
        in_specs=[
            pl.BlockSpec((1, tm, d), lambda b, m: (b, m, 0)),
            pl.BlockSpec((1, 1, d), lambda b, m: (b, 0, 0)),
            pl.BlockSpec((1, 1, d), lambda b, m: (b, 0, 0)),
            pl.BlockSpec((1, d), lambda b, m: (0, 0)),
            pl.BlockSpec((LANES, d), lambda b, m: (0, 0)),
            pl.BlockSpec((1, LANES), lambda b, m: (0, 0)),
        ],
        out_specs=[
            pl.BlockSpec((1, tm, d), lambda b, m: (b, m, 0)),
            pl.BlockSpec((1, tm, LANES), lambda b, m: (b, m, 0)),
            pl.BlockSpec((1, HEADS, tm), lambda b, m: (b, 0, m)),
        ],
        out_shape=[
            jax.ShapeDtypeStruct((bsz, s, d), jnp.bfloat16),
            jax.ShapeDtypeStruct((bsz, s, LANES), jnp.float32),
            jax.ShapeDtypeStruct((bsz, HEADS, s), jnp.float32),
        ],
        compiler_params=pltpu.CompilerParams(
            dimension_semantics=("arbitrary", "arbitrary"), vmem_limit_bytes=VMEM_LIMIT),
        name="modnorm_gates",
    )(x, scale, shift, norm_w, wt_gate, b_gate)


def _load_weight_block(wt_ref, w_scr, col0):
    step = 256
    for r in range(0, wt_ref.shape[0], step):
        w_scr[:, col0 + r:col0 + r + step] = wt_ref[r:r + step, :].T.astype(w_scr.dtype)


def _proj_a_kernel(h_ref, wt_ref, b_ref, wa_ref, wb_ref, wo_ref,
                   o_ref, wa16_ref, wb16_ref, wo16_ref, w_scr):
    @pl.when((pl.program_id(1) == 0) & (pl.program_id(2) == 0))
    def _():
        _load_weight_block(wt_ref, w_scr, 0)

    acc = jnp.dot(h_ref[0], w_scr[...], preferred_element_type=jnp.float32)
    o_ref[0] = (acc + b_ref[...]).astype(o_ref.dtype)
    wa16_ref[...] = wa_ref[...].astype(wa16_ref.dtype)
    wb16_ref[...] = wb_ref[...].astype(wb16_ref.dtype)
    wo16_ref[...] = wo_ref[...].astype(wo16_ref.dtype)


def _proj_a_call(h, wt, b_a, w_proj_a, w_proj_b, w_out):
    bsz, s, d = h.shape
    n = MLSTM_WIDTH
    tm, tn = PROJA_TM, PROJA_TN
    nb, nm = bsz, s // tm
    steps = (n // tn) * nb * nm
    slab = d // steps
    assert slab * steps == d and slab % 16 == 0
    slab_blk = pl.BlockSpec((slab, d), lambda j, b, m: ((j * nb + b) * nm + m, 0))
    w16_sds = jax.ShapeDtypeStruct((d, d), jnp.bfloat16)
    return pl.pallas_call(
        _proj_a_kernel,
        grid=(n // tn, nb, nm),
        in_specs=[
            pl.BlockSpec((1, tm, d), lambda j, b, m: (b, m, 0)),
            pl.BlockSpec((tn, d), lambda j, b, m: (j, 0)),
            pl.BlockSpec((1, tn), lambda j, b, m: (0, j)),
            slab_blk, slab_blk, slab_blk,
        ],
        out_specs=[pl.BlockSpec((1, tm, tn), lambda j, b, m: (b, m, j)),
                   slab_blk, slab_blk, slab_blk],
        out_shape=[jax.ShapeDtypeStruct((bsz, s, n), jnp.bfloat16), w16_sds, w16_sds, w16_sds],
        scratch_shapes=[pltpu.VMEM((d, tn), jnp.bfloat16)],
        compiler_params=pltpu.CompilerParams(
            dimension_semantics=("arbitrary", "arbitrary", "arbitrary"),
            vmem_limit_bytes=VMEM_LIMIT),
        name="proj_mlstm",
    )(h, wt, b_a, w_proj_a, w_proj_b, w_out)


def _mlstm_chunk(qh, kh, vh, og, zg, b, a, cm, a_row, hw, c_scr, n_scr, m_scr):
    L = qh.shape[0]
    m_state = m_scr[...]
    big_m = jnp.maximum(m_state, cm)
    w_inter = jnp.exp(m_state - big_m) * (QK_DIM ** -0.5)
    clamp = jnp.exp(-b - big_m)
    m_last = big_m[L - 1:L, :]
    ws = jnp.exp(a - m_last)
    keep = jnp.exp(m_state - m_last)
    m_scr[...] = b[L - 1:L, :] + m_last
    neg_m = math.log(QK_DIM ** -0.5) - big_m

    ri = lax.broadcasted_iota(jnp.int32, (L, L), 0)
    ci = lax.broadcasted_iota(jnp.int32, (L, L), 1)
    c_h = c_scr[...]
    n_h = n_scr[...]

    qk = lax.dot_general(qh, kh, (((1,), (1,)), ((), ())), preferred_element_type=jnp.float32)
    scores = jnp.where(ci <= ri, jnp.exp(a_row + neg_m), 0.0) * qk
    num = jnp.dot(scores.astype(jnp.bfloat16), vh, preferred_element_type=jnp.float32)
    num = num + w_inter * jnp.dot(qh, c_h.astype(jnp.bfloat16), preferred_element_type=jnp.float32)
    den = jnp.sum(scores, axis=-1, keepdims=True) \
        + w_inter * jnp.sum(qh.astype(jnp.float32) * n_h, axis=-1, keepdims=True)
    h_out = num * (1.0 / jnp.maximum(jnp.abs(den), clamp))

    vw = (ws * vh.astype(jnp.float32)).astype(jnp.bfloat16)
    kv = lax.dot_general(kh, vw, (((0,), (0,)), ((), ())), preferred_element_type=jnp.float32)
    c_scr[...] = keep * c_h + kv
    n_scr[...] = keep * n_h + jnp.sum(ws * kh.astype(jnp.float32), axis=0, keepdims=True)

    hg = _sigmoid(og.astype(jnp.float32)) * h_out
    ms = jnp.mean(hg * hg, axis=-1, keepdims=True)
    return hg * lax.rsqrt(ms + EPS) * hw * _silu(zg.astype(jnp.float32))


def _proj_b_kernel(h_ref, wu_ref, wbg_ref, wcg_ref, wzb_ref, wga_ref, wgb_ref, b_ref, cw_ref,
                   q_ref, k_ref, v_ref, o_ref, z_ref, gcols_ref, gat_ref, hw_ref,
                   ha_ref, hb_ref, sga_ref, sgb_ref,
                   w_scr, c_scr, n_scr, m_scr, carry_scr):
    head = pl.program_id(0)
    cb = CONV_CB

    @pl.when((pl.program_id(1) == 0) & (pl.program_id(2) == 0))
    def _():
        for g, win in enumerate((wu_ref, wbg_ref, wcg_ref, wzb_ref, wga_ref, wgb_ref)):
            _load_weight_block(win, w_scr, g * cb)

    @pl.when(pl.program_id(2) == 0)
    def _():
        c_scr[...] = jnp.zeros_like(c_scr)
        n_scr[...] = jnp.zeros_like(n_scr)
        m_scr[...] = jnp.zeros_like(m_scr)
        carry_scr[...] = jnp.zeros_like(carry_scr)

    L = MLSTM_L
    lane = lax.broadcasted_iota(jnp.int32, (L, LANES), 1)
    row = lax.broadcasted_iota(jnp.int32, (L, cb), 0)

    def head_col(rows, group):
        return jnp.sum(jnp.where(lane == head + group * HEADS, gcols_ref[0, rows, :], 0.0),
                       axis=-1, keepdims=True)

    def project(rows, half):
        cols = slice(half * 3 * cb, (half + 1) * 3 * cb)
        acc = jnp.dot(h_ref[0, rows, :], w_scr[:, cols], preferred_element_type=jnp.float32)
        return acc + b_ref[0, :, cols]

    for cc in range(h_ref.shape[1] // L):
        rows = slice(cc * L, (cc + 1) * L)
        p1 = project(rows, 0)
        cu = p1[:, 2 * cb:3 * cb] * p1[:, 0:cb]
        prev = carry_scr[...]
        cu1 = jnp.where(row >= 1, pltpu.roll(cu, 1, 0), prev[7:8, :])
        cu2 = jnp.where(row >= 2, pltpu.roll(cu, 2, 0),
                        jnp.where(row == 1, prev[7:8, :], prev[6:7, :]))
        carry_scr[...] = cu[L - 8:, :]
        conv = cw_ref[0:1, :] * cu2 + cw_ref[1:2, :] * cu1 + cw_ref[2:3, :] * cu
        bconv = p1[:, cb:2 * cb] * conv

        out = _mlstm_chunk(q_ref[0, rows, :], k_ref[0, rows, :], v_ref[0, rows, :],
                           o_ref[0, rows, :], z_ref[0, rows, :],
                           head_col(rows, 0), head_col(rows, 1), head_col(rows, 2),
                           gat_ref[0, pl.ds(head, 1), rows], hw_ref[...], c_scr, n_scr, m_scr)
        ha_ref[0, rows, :] = out.astype(ha_ref.dtype)

        p2 = project(rows, 1)
        hb_ref[0, rows, :] = (bconv * _silu(p2[:, 0:cb])).astype(hb_ref.dtype)
        sga_ref[0, rows, :] = _sigmoid(p2[:, cb:2 * cb]).astype(sga_ref.dtype)
        sgb_ref[0, rows, :] = _sigmoid(p2[:, 2 * cb:3 * cb]).astype(sgb_ref.dtype)


def _proj_b_call(h, wt, b_perm, conv_w, proj_a, g_cols, g_at, headnorm_w):
    bsz, s, d = h.shape
    tm = PROJ_TM
    cb = CONV_CB
    rest_off = MLSTM_WIDTH + 2 * HEADS
    assert rest_off % SUBLANES == 0 and cb % SUBLANES == 0 and d % SUBLANES == 0

    def w_win(g):
        return pl.BlockSpec(
            (pl.Element(cb), pl.Element(d)),
            lambda hd, b, m: (((rest_off + g * d) // SUBLANES + hd * (cb // SUBLANES)) * SUBLANES, 0))

    qk_blk = lambda off: pl.BlockSpec((1, tm, QK_DIM), lambda hd, b, m: (b, m, off + hd))
    v_blk = lambda off: pl.BlockSpec((1, tm, V_DIM), lambda hd, b, m: (b, m, off + hd))
    out_blk = pl.BlockSpec((1, tm, cb), lambda hd, b, m: (b, m, hd))
    out_sds = jax.ShapeDtypeStruct((bsz, s, d), jnp.bfloat16)
    return pl.pallas_call(
        _proj_b_kernel,
        grid=(HEADS, bsz, s // tm),
        in_specs=[
            pl.BlockSpec((1, tm, d), lambda hd, b, m: (b, m, 0)),
            w_win(0), w_win(1), w_win(2), w_win(3), w_win(4), w_win(5),
            pl.BlockSpec((1, 1, 6 * cb), lambda hd, b, m: (hd, 0, 0)),
            pl.BlockSpec((CONV_K, cb), lambda hd, b, m: (0, hd)),
            qk_blk(0), qk_blk(HEADS),
            v_blk(HEADS), v_blk(2 * HEADS), v_blk(3 * HEADS),
            pl.BlockSpec((1, tm, LANES), lambda hd, b, m: (b, m, 0)),
            pl.BlockSpec((1, HEADS, tm), lambda hd, b, m: (b, 0, m)),
            pl.BlockSpec((1, V_DIM), lambda hd, b, m: (0, hd)),
        ],
        out_specs=[out_blk, out_blk, out_blk, out_blk],
        out_shape=[out_sds, out_sds, out_sds, out_sds],
        scratch_shapes=[
            pltpu.VMEM((d, 6 * cb), jnp.bfloat16),
            pltpu.VMEM((QK_DIM, V_DIM), jnp.float32),
            pltpu.VMEM((1, QK_DIM), jnp.float32),
            pltpu.VMEM((1, 1), jnp.float32),
            pltpu.VMEM((8, cb), jnp.float32),
        ],
        compiler_params=pltpu.CompilerParams(
            dimension_semantics=("arbitrary", "arbitrary", "arbitrary"),
            vmem_limit_bytes=VMEM_LIMIT),
        name="proj_conv_mlstm",
    )(h, wt, wt, wt, wt, wt, wt, b_perm, conv_w,
      proj_a, proj_a, proj_a, proj_a, proj_a, g_cols, g_at, headnorm_w)


def _merge_kernel(ha_ref, hb_ref, sga_ref, sgb_ref, x_ref, gate_ref,
                  wa_ref, wb_ref, wo_ref, nf_ref, out_ref, *, final_norm):
    y_a = jnp.dot(ha_ref[0], wa_ref[...], preferred_element_type=jnp.float32)
    y_b = jnp.dot(hb_ref[0], wb_ref[...], preferred_element_type=jnp.float32)
    merged = sga_ref[0].astype(jnp.float32) * y_a + sgb_ref[0].astype(jnp.float32) * y_b
    o = jnp.dot(merged.astype(jnp.bfloat16), wo_ref[...], preferred_element_type=jnp.float32)
    xo = x_ref[0] + gate_ref[0] * o
    if final_norm:
        ms = jnp.mean(xo * xo, axis=-1, keepdims=True)
        xo = xo * lax.rsqrt(ms + EPS) * nf_ref[...]
    out_ref[0] = xo


def _merge_call(h_a, h_b, sg_a, sg_b, x, gate, w_proj_a, w_proj_b, w_out, normf_w, final_norm):
    bsz, s, d = x.shape
    tm = MERGE_TM
    tile = pl.BlockSpec((1, tm, d), lambda b, t: (b, t, 0))
    const = lambda shape: pl.BlockSpec(shape, lambda b, t: (0,) * len(shape),
                                       pipeline_mode=pl.Buffered(1))
    return pl.pallas_call(
        functools.partial(_merge_kernel, final_norm=final_norm),
        grid=(bsz, s // tm),
        in_specs=[
            tile, tile, tile, tile,
            tile,
            pl.BlockSpec((1, 1, d), lambda b, t: (b, 0, 0)),
            const((V_WIDTH, d)), const((d, d)), const((d, d)),
            const((1, d)),
        ],
        out_specs=tile,
        out_shape=jax.ShapeDtypeStruct((bsz, s, d), jnp.float32),
        compiler_params=pltpu.CompilerParams(
            dimension_semantics=("arbitrary", "arbitrary"), vmem_limit_bytes=VMEM_LIMIT),
        name="merge_out",
    )(h_a, h_b, sg_a, sg_b, x, gate, w_proj_a, w_proj_b, w_out, normf_w)


def kernel(x, c, norm1_w, w_ada, b_ada, w_in, b_in, conv_w, headnorm_w,
           w_proj_a, w_proj_b, w_out, normf_w):
    bsz, s, d = x.shape
    depth = norm1_w.shape[0]
    gate_off = MLSTM_WIDTH
    rest_off = gate_off + 2 * HEADS
    bf16 = jnp.bfloat16

    c_pad = jnp.pad(c, ((0, 8 - bsz), (0, 0)))
    for l in range(depth):
        mod = _ada_call(c_pad, w_ada[l], b_ada[l][None, :])[:bsz]
        shift, scale, gate = (m[:, None, :] for m in jnp.split(mod, 3, axis=-1))

        wt, b_l = jnp.swapaxes(w_in[l], 0, 1), b_in[l]
        gate_pad = LANES - 2 * HEADS
        h, g_cols, g_at = _norm_call(
            x, scale, shift, norm1_w[l][None, :],
            jnp.pad(wt[gate_off:rest_off], ((0, gate_pad), (0, 0))),
            jnp.pad(b_l[None, gate_off:rest_off], ((0, 0), (0, gate_pad))))
        assert w_proj_a[l].shape == w_proj_b[l].shape == w_out[l].shape == (d, d)
        proj_a, wa16, wb16, wo16 = _proj_a_call(h, wt, b_l[None, :gate_off],
                                                w_proj_a[l], w_proj_b[l], w_out[l])
        b_perm = b_l[rest_off:].reshape(6, d // CONV_CB, CONV_CB).transpose(1, 0, 2)
        b_perm = b_perm.reshape(d // CONV_CB, 1, 6 * CONV_CB)
        h_a, h_b, sg_a, sg_b = _proj_b_call(h, wt, b_perm, conv_w[l], proj_a, g_cols, g_at,
                                            headnorm_w[l][None, :])
        x = _merge_call(h_a, h_b, sg_a, sg_b, x, gate, wa16, wb16, wo16, normf_w[None, :],
                        final_norm=(l == depth - 1))
    return x
```

```python
import functools
import math

import jax
import jax.numpy as jnp
from jax import lax
from jax.experimental import pallas as pl
from jax.experimental.pallas import tpu as pltpu

HEADS = 8
QK_DIM = 128
V_DIM = 256
QK_WIDTH = HEADS * QK_DIM
V_WIDTH = HEADS * V_DIM
MLSTM_WIDTH = 2 * QK_WIDTH + 3 * V_WIDTH
CONV_K = 3
EPS = 1e-6

LANES = 128
SUBLANES = 8

ADA_TN = 768
NORM_TM = 1024
PROJ_TM = 2048
PROJA_TM = 2048
PROJA_TN = 1024
MLSTM_L = 256
MERGE_TM = 256
CONV_CB = V_DIM

VMEM_LIMIT = 58 * 1024 * 1024


def _sigmoid(v):
    return 0.5 * jnp.tanh(0.5 * v) + 0.5


def _silu(v):
    return v * _sigmoid(v)


def _ada_kernel(c_ref, w_ref, b_ref, o_ref):
    c_act = _silu(c_ref[...])
    o_ref[...] = jnp.dot(c_act, w_ref[...], preferred_element_type=jnp.float32) + b_ref[...]


def _ada_call(c_pad, w_ada, b_ada):
    rows, d = c_pad.shape
    n = w_ada.shape[1]
    return pl.pallas_call(
        _ada_kernel,
        grid=(n // ADA_TN,),
        in_specs=[
            pl.BlockSpec((rows, d), lambda j: (0, 0)),
            pl.BlockSpec((d, ADA_TN), lambda j: (0, j)),
            pl.BlockSpec((1, ADA_TN), lambda j: (0, j)),
        ],
        out_specs=pl.BlockSpec((rows, ADA_TN), lambda j: (0, j)),
        out_shape=jax.ShapeDtypeStruct((rows, n), jnp.float32),
        compiler_params=pltpu.CompilerParams(
            dimension_semantics=("arbitrary",), vmem_limit_bytes=VMEM_LIMIT),
        name="adaln_mod",
    )(c_pad, w_ada, b_ada)


def _chunk_scan_rows(v, combine, fill, chunk):
    pos = lax.broadcasted_iota(jnp.int32, v.shape, 0) % chunk
    step = 1
    while step < chunk:
        shifted = pltpu.roll(v, step, 0)
        v = combine(v, jnp.where(pos >= step, shifted, fill))
        step *= 2
    return v


def _norm_kernel(x_ref, scale_ref, shift_ref, nw_ref, wg_ref, bg_ref, h_ref, cols_ref, at_ref):
    xf = x_ref[0]
    ms = jnp.mean(xf * xf, axis=-1, keepdims=True)
    y = xf * lax.rsqrt(ms + EPS) * nw_ref[...]
    h = (y * (1.0 + scale_ref[0]) + shift_ref[0]).astype(jnp.bfloat16)
    h_ref[0] = h
    g = lax.dot_general(h, wg_ref[...].astype(jnp.bfloat16), (((1,), (1,)), ((), ())),
                        preferred_element_type=jnp.float32) + bg_ref[...]
    lf = jax.nn.log_sigmoid(pltpu.roll(g, LANES - HEADS, 1))
    b = _chunk_scan_rows(lf, jnp.add, 0.0, MLSTM_L)
    a = g - b
    cm = _chunk_scan_rows(a, jnp.maximum, -jnp.inf, MLSTM_L)
    lane = lax.broadcasted_iota(jnp.int32, g.shape, 1)
    cols_ref[0] = jnp.where(lane < HEADS, b,
                            jnp.where(lane < 2 * HEADS, pltpu.roll(a, HEADS, 1),
                                      pltpu.roll(cm, 2 * HEADS, 1)))
    at_ref[0] = a.T[:HEADS, :]


def _norm_call(x, scale, shift, norm_w, wt_gate, b_gate):
    bsz, s, d = x.shape
    tm = NORM_TM
    return pl.pallas_call(
        _norm_kernel,
        grid=(bsz, s // tm),
        in_specs=[
            pl.BlockSpec((1, tm, d), lambda b, m: (b, m, 0)),
            pl.BlockSpec((1, 1, d), lambda b, m: (b, 0, 0)),
            pl.BlockSpec((1, 1, d), lambda b, m: (b, 0, 0)),
            pl.BlockSpec((1, d), lambda b, m: (0, 0)),
            pl.BlockSpec((LANES, d), lambda b, m: (0, 0)),
            pl.BlockSpec((1, LANES), lambda b, m: (0, 0)),
        ],
        out_specs=[
            pl.BlockSpec((1, tm, d), lambda b, m: (b, m, 0)),
            pl.BlockSpec((1, tm, LANES), lambda b, m: (b, m, 0)),
            pl.BlockSpec((1, HEADS, tm), lambda b, m: (b, 0, m)),
        ],
        out_shape=[
            jax.ShapeDtypeStruct((bsz, s, d), jnp.bfloat16),
            jax.ShapeDtypeStruct((bsz, s, LANES), jnp.float32),
            jax.ShapeDtypeStruct((bsz, HEADS, s), jnp.float32),
        ],
        compiler_params=pltpu.CompilerParams(
            dimension_semantics=("arbitrary", "arbitrary"), vmem_limit_bytes=VMEM_LIMIT),
        name="modnorm_gates",
    )(x, scale, shift, norm_w, wt_gate, b_gate)


def _load_weight_block(wt_ref, w_scr, col0):
    step = 256
    for r in range(0, wt_ref.shape[0], step):
        w_scr[:, col0 + r:col0 + r + step] = wt_ref[r:r + step, :].T.astype(w_scr.dtype)


def _proj_a_kernel(h_ref, wt_ref, b_ref, wa_ref, wb_ref, wo_ref,
                   o_ref, wa16_ref, wb16_ref, wo16_ref, w_scr):
    @pl.when((pl.program_id(1) == 0) & (pl.program_id(2) == 0))
    def _():
        _load_weight_block(wt_ref, w_scr, 0)

    acc = jnp.dot(h_ref[0], w_scr[...], preferred_element_type=jnp.float32)
    o_ref[0] = (acc + b_ref[...]).astype(o_ref.dtype)
    wa16_ref[...] = wa_ref[...].astype(wa16_ref.dtype)
    wb16_ref[...] = wb_ref[...].astype(wb16_ref.dtype)
    wo16_ref[...] = wo_ref[...].astype(wo16_ref.dtype)


def _proj_a_call(h, wt, b_a, w_proj_a, w_proj_b, w_out):
    bsz, s, d = h.shape
    n = MLSTM_WIDTH
    tm, tn = PROJA_TM, PROJA_TN
    nb, nm = bsz, s // tm
    steps = (n // tn) * nb * nm
    slab = d // steps
    assert slab * steps == d and slab % 16 == 0
    slab_blk = pl.BlockSpec((slab, d), lambda j, b, m: ((j * nb + b) * nm + m, 0))
    w16_sds = jax.ShapeDtypeStruct((d, d), jnp.bfloat16)
    return pl.pallas_call(
        _proj_a_kernel,
        grid=(n // tn, nb, nm),
        in_specs=[
            pl.BlockSpec((1, tm, d), lambda j, b, m: (b, m, 0)),
            pl.BlockSpec((tn, d), lambda j, b, m: (j, 0)),
            pl.BlockSpec((1, tn), lambda j, b, m: (0, j)),
            slab_blk, slab_blk, slab_blk,
        ],
        out_specs=[pl.BlockSpec((1, tm, tn), lambda j, b, m: (b, m, j)),
                   slab_blk, slab_blk, slab_blk],
        out_shape=[jax.ShapeDtypeStruct((bsz, s, n), jnp.bfloat16), w16_sds, w16_sds, w16_sds],
        scratch_shapes=[pltpu.VMEM((d, tn), jnp.bfloat16)],
        compiler_params=pltpu.CompilerParams(
            dimension_semantics=("arbitrary", "arbitrary", "arbitrary"),
            vmem_limit_bytes=VMEM_LIMIT),
        name="proj_mlstm",
    )(h, wt, b_a, w_proj_a, w_proj_b, w_out)


def _mlstm_chunk(qh, kh, vh, og, zg, b, a, cm, a_row, hw, c_scr, n_scr, m_scr):
    L = qh.shape[0]
    m_state = m_scr[...]
    big_m = jnp.maximum(m_state, cm)
    w_inter = jnp.exp(m_state - big_m) * (QK_DIM ** -0.5)
    clamp = jnp.exp(-b - big_m)
    m_last = big_m[L - 1:L, :]
    ws = jnp.exp(a - m_last)
    keep = jnp.exp(m_state - m_last)
    m_scr[...] = b[L - 1:L, :] + m_last
    neg_m = math.log(QK_DIM ** -0.5) - big_m

    ri = lax.broadcasted_iota(jnp.int32, (L, L), 0)
    ci = lax.broadcasted_iota(jnp.int32, (L, L), 1)
    c_h = c_scr[...]
    n_h = n_scr[...]

    qk = lax.dot_general(qh, kh, (((1,), (1,)), ((), ())), preferred_element_type=jnp.float32)
    scores = jnp.where(ci <= ri, jnp.exp(a_row + neg_m), 0.0) * qk
    num = jnp.dot(scores.astype(jnp.bfloat16), vh, preferred_element_type=jnp.float32)
    num = num + w_inter * jnp.dot(qh, c_h.astype(jnp.bfloat16), preferred_element_type=jnp.float32)
    den = jnp.sum(scores, axis=-1, keepdims=True) \
        + w_inter * jnp.sum(qh.astype(jnp.float32) * n_h, axis=-1, keepdims=True)
    h_out = num * (1.0 / jnp.maximum(jnp.abs(den), clamp))

    vw = (ws * vh.astype(jnp.float32)).astype(jnp.bfloat16)
    kv = lax.dot_general(kh, vw, (((0,), (0,)), ((), ())), preferred_element_type=jnp.float32)
    c_scr[...] = keep * c_h + kv
    n_scr[...] = keep * n_h + jnp.sum(ws * kh.astype(jnp.float32), axis=0, keepdims=True)

    hg = _sigmoid(og.astype(jnp.float32)) * h_out
    ms = jnp.mean(hg * hg, axis=-1, keepdims=True)
    return hg * lax.rsqrt(ms + EPS) * hw * _silu(zg.astype(jnp.float32))


N_GROUPS = 6


def _proj_b_kernel(h_ref, wt_hbm, b_ref, cw_ref,
                   q_ref, k_ref, v_ref, o_ref, z_ref, gcols_ref, gat_ref, hw_ref,
                   ha_ref, hb_ref, sga_ref, sgb_ref,
                   w_scr, stage, sem, c_scr, n_scr, m_scr, carry_scr, *, rest_off):
    head = pl.program_id(0)
    cb = CONV_CB
    d = stage.shape[1]
    steps_per_head = pl.num_programs(1) * pl.num_programs(2)
    t = pl.program_id(1) * pl.num_programs(2) + pl.program_id(2)
    slot = head % 2

    def window_copy(hd, g):
        row0 = pl.multiple_of(rest_off + g * d + hd * cb, SUBLANES)
        return pltpu.make_async_copy(wt_hbm.at[pl.ds(row0, cb), :], stage, sem)

    def store_window(dst_slot, g):
        step = 256
        for r in range(0, cb, step):
            w_scr[dst_slot, :, g * cb + r:g * cb + r + step] = stage[r:r + step, :].T.astype(w_scr.dtype)

    @pl.when((head == 0) & (t == 0))
    def _():
        for g in range(N_GROUPS):
            cp = window_copy(0, g)
            cp.start()
            cp.wait()
            store_window(0, g)

    g_next = t - (steps_per_head - N_GROUPS)
    prefetch = (g_next >= 0) & (head + 1 < pl.num_programs(0))

    @pl.when(prefetch)
    def _():
        window_copy(head + 1, g_next).start()

    @pl.when(pl.program_id(2) == 0)
    def _():
        c_scr[...] = jnp.zeros_like(c_scr)
        n_scr[...] = jnp.zeros_like(n_scr)
        m_scr[...] = jnp.zeros_like(m_scr)
        carry_scr[...] = jnp.zeros_like(carry_scr)

    L = MLSTM_L
    lane = lax.broadcasted_iota(jnp.int32, (L, LANES), 1)
    row = lax.broadcasted_iota(jnp.int32, (L, cb), 0)

    def head_col(rows, group):
        return jnp.sum(jnp.where(lane == head + group * HEADS, gcols_ref[0, rows, :], 0.0),
                       axis=-1, keepdims=True)

    def project(rows, half):
        cols = slice(half * 3 * cb, (half + 1) * 3 * cb)
        acc = jnp.dot(h_ref[0, rows, :], w_scr[slot, :, cols], preferred_element_type=jnp.float32)
        return acc + b_ref[0, :, cols]

    for cc in range(h_ref.shape[1] // L):
        rows = slice(cc * L, (cc + 1) * L)
        p1 = project(rows, 0)
        cu = p1[:, 2 * cb:3 * cb] * p1[:, 0:cb]
        prev = carry_scr[...]
        cu1 = jnp.where(row >= 1, pltpu.roll(cu, 1, 0), prev[7:8, :])
        cu2 = jnp.where(row >= 2, pltpu.roll(cu, 2, 0),
                        jnp.where(row == 1, prev[7:8, :], prev[6:7, :]))
        carry_scr[...] = cu[L - 8:, :]
        conv = cw_ref[0:1, :] * cu2 + cw_ref[1:2, :] * cu1 + cw_ref[2:3, :] * cu
        bconv = p1[:, cb:2 * cb] * conv

        out = _mlstm_chunk(q_ref[0, rows, :], k_ref[0, rows, :], v_ref[0, rows, :],
                           o_ref[0, rows, :], z_ref[0, rows, :],
                           head_col(rows, 0), head_col(rows, 1), head_col(rows, 2),
                           gat_ref[0, pl.ds(head, 1), rows], hw_ref[...], c_scr, n_scr, m_scr)
        ha_ref[0, rows, :] = out.astype(ha_ref.dtype)

        p2 = project(rows, 1)
        hb_ref[0, rows, :] = (bconv * _silu(p2[:, 0:cb])).astype(hb_ref.dtype)
        sga_ref[0, rows, :] = _sigmoid(p2[:, cb:2 * cb]).astype(sga_ref.dtype)
        sgb_ref[0, rows, :] = _sigmoid(p2[:, 2 * cb:3 * cb]).astype(sgb_ref.dtype)

    @pl.when(prefetch)
    def _():
        window_copy(head + 1, g_next).wait()

    for g in range(N_GROUPS):
        @pl.when(prefetch & (g_next == g))
        def _(g=g):
            store_window(1 - slot, g)


def _proj_b_call(h, wt, b_perm, conv_w, proj_a, g_cols, g_at, headnorm_w):
    bsz, s, d = h.shape
    tm = PROJ_TM
    cb = CONV_CB
    rest_off = MLSTM_WIDTH + 2 * HEADS
    assert rest_off % SUBLANES == 0 and cb % SUBLANES == 0 and d % SUBLANES == 0
    assert bsz * (s // tm) >= N_GROUPS

    qk_blk = lambda off: pl.BlockSpec((1, tm, QK_DIM), lambda hd, b, m: (b, m, off + hd))
    v_blk = lambda off: pl.BlockSpec((1, tm, V_DIM), lambda hd, b, m: (b, m, off + hd))
    out_blk = pl.BlockSpec((1, tm, cb), lambda hd, b, m: (b, m, hd))
    out_sds = jax.ShapeDtypeStruct((bsz, s, d), jnp.bfloat16)
    return pl.pallas_call(
        functools.partial(_proj_b_kernel, rest_off=rest_off),
        grid=(HEADS, bsz, s // tm),
        in_specs=[
            pl.BlockSpec((1, tm, d), lambda hd, b, m: (b, m, 0)),
            pl.BlockSpec(memory_space=pl.ANY),
            pl.BlockSpec((1, 1, N_GROUPS * cb), lambda hd, b, m: (hd, 0, 0)),
            pl.BlockSpec((CONV_K, cb), lambda hd, b, m: (0, hd)),
            qk_blk(0), qk_blk(HEADS),
            v_blk(HEADS), v_blk(2 * HEADS), v_blk(3 * HEADS),
            pl.BlockSpec((1, tm, LANES), lambda hd, b, m: (b, m, 0)),
            pl.BlockSpec((1, HEADS, tm), lambda hd, b, m: (b, 0, m)),
            pl.BlockSpec((1, V_DIM), lambda hd, b, m: (0, hd)),
        ],
        out_specs=[out_blk, out_blk, out_blk, out_blk],
        out_shape=[out_sds, out_sds, out_sds, out_sds],
        scratch_shapes=[
            pltpu.VMEM((2, d, N_GROUPS * cb), jnp.bfloat16),
            pltpu.VMEM((cb, d), jnp.float32),
            pltpu.SemaphoreType.DMA(()),
            pltpu.VMEM((QK_DIM, V_DIM), jnp.float32),
            pltpu.VMEM((1, QK_DIM), jnp.float32),
            pltpu.VMEM((1, 1), jnp.float32),
            pltpu.VMEM((8, cb), jnp.float32),
        ],
        compiler_params=pltpu.CompilerParams(
            dimension_semantics=("arbitrary", "arbitrary", "arbitrary"),
            vmem_limit_bytes=VMEM_LIMIT),
        name="proj_conv_mlstm",
    )(h, wt, b_perm, conv_w,
      proj_a, proj_a, proj_a, proj_a, proj_a, g_cols, g_at, headnorm_w)


def _merge_kernel(ha_ref, hb_ref, sga_ref, sgb_ref, x_ref, gate_ref,
                  wa_ref, wb_ref, wo_ref, nf_ref, out_ref, *, final_norm):
    y_a = jnp.dot(ha_ref[0], wa_ref[...], preferred_element_type=jnp.float32)
    y_b = jnp.dot(hb_ref[0], wb_ref[...], preferred_element_type=jnp.float32)
    merged = sga_ref[0].astype(jnp.float32) * y_a + sgb_ref[0].astype(jnp.float32) * y_b
    o = jnp.dot(merged.astype(jnp.bfloat16), wo_ref[...], preferred_element_type=jnp.float32)
    xo = x_ref[0] + gate_ref[0] * o
    if final_norm:
        ms = jnp.mean(xo * xo, axis=-1, keepdims=True)
        xo = xo * lax.rsqrt(ms + EPS) * nf_ref[...]
    out_ref[0] = xo


def _merge_call(h_a, h_b, sg_a, sg_b, x, gate, w_proj_a, w_proj_b, w_out, normf_w, final_norm):
    bsz, s, d = x.shape
    tm = MERGE_TM
    tile = pl.BlockSpec((1, tm, d), lambda b, t: (b, t, 0))
    const = lambda shape: pl.BlockSpec(shape, lambda b, t: (0,) * len(shape),
                                       pipeline_mode=pl.Buffered(1))
    return pl.pallas_call(
        functools.partial(_merge_kernel, final_norm=final_norm),
        grid=(bsz, s // tm),
        in_specs=[
            tile, tile, tile, tile,
            tile,
            pl.BlockSpec((1, 1, d), lambda b, t: (b, 0, 0)),
            const((V_WIDTH, d)), const((d, d)), const((d, d)),
            const((1, d)),
        ],
        out_specs=tile,
        out_shape=jax.ShapeDtypeStruct((bsz, s, d), jnp.float32),
        compiler_params=pltpu.CompilerParams(
            dimension_semantics=("arbitrary", "arbitrary"), vmem_limit_bytes=VMEM_LIMIT),
        name="merge_out",
    )(h_a, h_b, sg_a, sg_b, x, gate, w_proj_a, w_proj_b, w_out, normf_w)


def kernel(x, c, norm1_w, w_ada, b_ada, w_in, b_in, conv_w, headnorm_w,
           w_proj_a, w_proj_b, w_out, normf_w):
    bsz, s, d = x.shape
    depth = norm1_w.shape[0]
    gate_off = MLSTM_WIDTH
    rest_off = gate_off + 2 * HEADS
    bf16 = jnp.bfloat16

    c_pad = jnp.pad(c, ((0, 8 - bsz), (0, 0)))
    for l in range(depth):
        mod = _ada_call(c_pad, w_ada[l], b_ada[l][None, :])[:bsz]
        shift, scale, gate = (m[:, None, :] for m in jnp.split(mod, 3, axis=-1))

        wt, b_l = jnp.swapaxes(w_in[l], 0, 1), b_in[l]
        gate_pad = LANES - 2 * HEADS
        h, g_cols, g_at = _norm_call(
            x, scale, shift, norm1_w[l][None, :],
            jnp.pad(wt[gate_off:rest_off], ((0, gate_pad), (0, 0))),
            jnp.pad(b_l[None, gate_off:rest_off], ((0, 0), (0, gate_pad))))
        assert w_proj_a[l].shape == w_proj_b[l].shape == w_out[l].shape == (d, d)
        proj_a, wa16, wb16, wo16 = _proj_a_call(h, wt, b_l[None, :gate_off],
                                                w_proj_a[l], w_proj_b[l], w_out[l])
        b_perm = b_l[rest_off:].reshape(6, d // CONV_CB, CONV_CB).transpose(1, 0, 2)
        b_perm = b_perm.reshape(d // CONV_CB, 1, 6 * CONV_CB)
        h_a, h_b, sg_a, sg_b = _proj_b_call(h, wt, b_perm, conv_w[l], proj_a, g_cols, g_at,
                                            headnorm_w[l][None, :])
        x = _merge_call(h_a, h_b, sg_a, sg_b, x, gate, wa16, wb16, wo16, normf_w[None, :],
                        final_norm=(l == depth - 1))
    return x
```

```python
import functools
import math

import jax
import jax.numpy as jnp
from jax import lax
from jax.experimental import pallas as pl
from jax.experimental.pallas import tpu as pltpu

HEADS = 8
QK_DIM = 128
V_DIM = 256
QK_WIDTH = HEADS * QK_DIM
V_WIDTH = HEADS * V_DIM
MLSTM_WIDTH = 2 * QK_WIDTH + 3 * V_WIDTH
CONV_K = 3
EPS = 1e-6

LANES = 128
SUBLANES = 8

ADA_TN = 768
NORM_TM = 512
PROJ_TM = 2048
PROJA_TM = 2048
PROJA_TN = 1024
MLSTM_L = 256
MERGE_TM = 256
CONV_CB = V_DIM

VMEM_LIMIT = 58 * 1024 * 1024


def _sigmoid(v):
    return 0.5 * jnp.tanh(0.5 * v) + 0.5


def _silu(v):
    return v * _sigmoid(v)


def _ada_kernel(c_ref, w_ref, b_ref, o_ref):
    c_act = _silu(c_ref[...])
    o_ref[...] = jnp.dot(c_act, w_ref[...], preferred_element_type=jnp.float32) + b_ref[...]


def _ada_call(c_pad, w_ada, b_ada):
    rows, d = c_pad.shape
    n = w_ada.shape[1]
    return pl.pallas_call(
        _ada_kernel,
        grid=(n // ADA_TN,),
        in_specs=[
            pl.BlockSpec((rows, d), lambda j: (0, 0)),
            pl.BlockSpec((d, ADA_TN), lambda j: (0, j)),
            pl.BlockSpec((1, ADA_TN), lambda j: (0, j)),
        ],
        out_specs=pl.BlockSpec((rows, ADA_TN), lambda j: (0, j)),
        out_shape=jax.ShapeDtypeStruct((rows, n), jnp.float32),
        compiler_params=pltpu.CompilerParams(
            dimension_semantics=("arbitrary",), vmem_limit_bytes=VMEM_LIMIT),
        name="adaln_mod",
    )(c_pad, w_ada, b_ada)


def _chunk_scan_rows(v, combine, fill, chunk):
    pos = lax.broadcasted_iota(jnp.int32, v.shape, 0) % chunk
    step = 1
    while step < chunk:
        shifted = pltpu.roll(v, step, 0)
        v = combine(v, jnp.where(pos >= step, shifted, fill))
        step *= 2
    return v


def _load_weight_block(wt_ref, w_scr, col0):
    step = 256
    for r in range(0, wt_ref.shape[0], step):
        w_scr[:, col0 + r:col0 + r + step] = wt_ref[r:r + step, :].T.astype(w_scr.dtype)


def _norm_q_kernel(x_ref, scale_ref, shift_ref, nw_ref, wg_ref, bg_ref, wt_ref, bq_ref,
                   wa_ref, wb_ref, wo_ref,
                   h_ref, cols_ref, at_ref, q_ref, wa16_ref, wb16_ref, wo16_ref,
                   w_scr, h_scr):
    s = pl.program_id(0)
    slot = s % 2
    half = w_scr.shape[1] // 2

    @pl.when(s == 0)
    def _():
        _load_weight_block(wt_ref, w_scr, 0)
        h_scr[1] = jnp.zeros(h_scr.shape[1:], h_scr.dtype)

    h_prev = h_scr[1 - slot]
    q_ref[0, :, :half] = (jnp.dot(h_prev, w_scr[:, :half], preferred_element_type=jnp.float32)
                          + bq_ref[:, :half]).astype(q_ref.dtype)

    xf = x_ref[0]
    ms = jnp.mean(xf * xf, axis=-1, keepdims=True)
    y = xf * lax.rsqrt(ms + EPS) * nw_ref[...]
    h = (y * (1.0 + scale_ref[0]) + shift_ref[0]).astype(jnp.bfloat16)
    h_ref[0] = h
    h_scr[slot] = h
    g = lax.dot_general(h, wg_ref[...].astype(jnp.bfloat16), (((1,), (1,)), ((), ())),
                        preferred_element_type=jnp.float32) + bg_ref[...]
    lf = jax.nn.log_sigmoid(pltpu.roll(g, LANES - HEADS, 1))
    b = _chunk_scan_rows(lf, jnp.add, 0.0, MLSTM_L)
    a = g - b
    cm = _chunk_scan_rows(a, jnp.maximum, -jnp.inf, MLSTM_L)
    lane = lax.broadcasted_iota(jnp.int32, g.shape, 1)
    cols_ref[0] = jnp.where(lane < HEADS, b,
                            jnp.where(lane < 2 * HEADS, pltpu.roll(a, HEADS, 1),
                                      pltpu.roll(cm, 2 * HEADS, 1)))
    at_ref[0] = a.T[:HEADS, :]

    q_ref[0, :, half:] = (jnp.dot(h_prev, w_scr[:, half:], preferred_element_type=jnp.float32)
                          + bq_ref[:, half:]).astype(q_ref.dtype)
    wa16_ref[...] = wa_ref[...].astype(wa16_ref.dtype)
    wb16_ref[...] = wb_ref[...].astype(wb16_ref.dtype)
    wo16_ref[...] = wo_ref[...].astype(wo16_ref.dtype)


def _norm_q_call(x, scale, shift, norm_w, wt_gate, b_gate, wt, b_q, w_proj_a, w_proj_b, w_out):
    bsz, s, d = x.shape
    tm, tn = NORM_TM, QK_WIDTH
    nm = s // tm
    tiles = bsz * nm
    slab = d // tiles
    assert slab * tiles == d and slab % 16 == 0
    cur = lambda i: jnp.minimum(i, tiles - 1)
    prv = lambda i: jnp.maximum(i - 1, 0)
    row_blk = lambda width, t: pl.BlockSpec((1, tm, width), lambda i: (t(i) // nm, t(i) % nm, 0))
    mod_blk = pl.BlockSpec((1, 1, d), lambda i: (cur(i) // nm, 0, 0))
    const = lambda shape: pl.BlockSpec(shape, lambda i: (0,) * len(shape), pipeline_mode=pl.Buffered(1))
    slab_blk = pl.BlockSpec((slab, d), lambda i: (cur(i), 0))
    w16_sds = jax.ShapeDtypeStruct((d, d), jnp.bfloat16)
    return pl.pallas_call(
        _norm_q_kernel,
        grid=(tiles + 1,),
        in_specs=[
            row_blk(d, cur), mod_blk, mod_blk, const((1, d)),
            const((LANES, d)), const((1, LANES)),
            const((tn, d)), const((1, tn)),
            slab_blk, slab_blk, slab_blk,
        ],
        out_specs=[
            row_blk(d, cur), row_blk(LANES, cur),
            pl.BlockSpec((1, HEADS, tm), lambda i: (cur(i) // nm, 0, cur(i) % nm)),
            row_blk(tn, prv),
            slab_blk, slab_blk, slab_blk,
        ],
        out_shape=[
            jax.ShapeDtypeStruct((bsz, s, d), jnp.bfloat16),
            jax.ShapeDtypeStruct((bsz, s, LANES), jnp.float32),
            jax.ShapeDtypeStruct((bsz, HEADS, s), jnp.float32),
            jax.ShapeDtypeStruct((bsz, s, tn), jnp.bfloat16),
            w16_sds, w16_sds, w16_sds,
        ],
        scratch_shapes=[pltpu.VMEM((d, tn), jnp.bfloat16), pltpu.VMEM((2, tm, d), jnp.bfloat16)],
        compiler_params=pltpu.CompilerParams(
            dimension_semantics=("arbitrary",), vmem_limit_bytes=VMEM_LIMIT),
        name="modnorm_q",
    )(x, scale, shift, norm_w, wt_gate, b_gate, wt, b_q, w_proj_a, w_proj_b, w_out)


def _proj_a_kernel(h_ref, wt_ref, b_ref, o_ref, w_scr):
    @pl.when((pl.program_id(1) == 0) & (pl.program_id(2) == 0))
    def _():
        _load_weight_block(wt_ref, w_scr, 0)

    acc = jnp.dot(h_ref[0], w_scr[...], preferred_element_type=jnp.float32)
    o_ref[0] = (acc + b_ref[...]).astype(o_ref.dtype)


def _proj_a_call(h, wt, b_rest):
    bsz, s, d = h.shape
    n = MLSTM_WIDTH - QK_WIDTH
    tm, tn = PROJA_TM, PROJA_TN
    first = QK_WIDTH // tn
    return pl.pallas_call(
        _proj_a_kernel,
        grid=(n // tn, bsz, s // tm),
        in_specs=[
            pl.BlockSpec((1, tm, d), lambda j, b, m: (b, m, 0)),
            pl.BlockSpec((tn, d), lambda j, b, m: (j + first, 0)),
            pl.BlockSpec((1, tn), lambda j, b, m: (0, j)),
        ],
        out_specs=pl.BlockSpec((1, tm, tn), lambda j, b, m: (b, m, j)),
        out_shape=jax.ShapeDtypeStruct((bsz, s, n), jnp.bfloat16),
        scratch_shapes=[pltpu.VMEM((d, tn), jnp.bfloat16)],
        compiler_params=pltpu.CompilerParams(
            dimension_semantics=("arbitrary", "arbitrary", "arbitrary"),
            vmem_limit_bytes=VMEM_LIMIT),
        name="proj_mlstm",
    )(h, wt, b_rest)


def _mlstm_chunk(qh, kh, vh, og, zg, b, a, cm, a_row, hw, c_scr, n_scr, m_scr):
    L = qh.shape[0]
    m_state = m_scr[...]
    big_m = jnp.maximum(m_state, cm)
    w_inter = jnp.exp(m_state - big_m) * (QK_DIM ** -0.5)
    clamp = jnp.exp(-b - big_m)
    m_last = big_m[L - 1:L, :]
    ws = jnp.exp(a - m_last)
    keep = jnp.exp(m_state - m_last)
    m_scr[...] = b[L - 1:L, :] + m_last
    neg_m = math.log(QK_DIM ** -0.5) - big_m

    ri = lax.broadcasted_iota(jnp.int32, (L, L), 0)
    ci = lax.broadcasted_iota(jnp.int32, (L, L), 1)
    c_h = c_scr[...]
    n_h = n_scr[...]

    qk = lax.dot_general(qh, kh, (((1,), (1,)), ((), ())), preferred_element_type=jnp.float32)
    scores = jnp.where(ci <= ri, jnp.exp(a_row + neg_m), 0.0) * qk
    num = jnp.dot(scores.astype(jnp.bfloat16), vh, preferred_element_type=jnp.float32)
    num = num + w_inter * jnp.dot(qh, c_h.astype(jnp.bfloat16), preferred_element_type=jnp.float32)
    den = jnp.sum(scores, axis=-1, keepdims=True) \
        + w_inter * jnp.sum(qh.astype(jnp.float32) * n_h, axis=-1, keepdims=True)
    h_out = num * (1.0 / jnp.maximum(jnp.abs(den), clamp))

    vw = (ws * vh.astype(jnp.float32)).astype(jnp.bfloat16)
    kv = lax.dot_general(kh, vw, (((0,), (0,)), ((), ())), preferred_element_type=jnp.float32)
    c_scr[...] = keep * c_h + kv
    n_scr[...] = keep * n_h + jnp.sum(ws * kh.astype(jnp.float32), axis=0, keepdims=True)

    hg = _sigmoid(og.astype(jnp.float32)) * h_out
    ms = jnp.mean(hg * hg, axis=-1, keepdims=True)
    return hg * lax.rsqrt(ms + EPS) * hw * _silu(zg.astype(jnp.float32))


N_GROUPS = 6


def _proj_b_kernel(h_ref, wt_hbm, b_ref, cw_ref,
                   q_ref, k_ref, v_ref, o_ref, z_ref, gcols_ref, gat_ref, hw_ref,
                   ha_ref, hb_ref, sga_ref, sgb_ref,
                   w_scr, stage, sem, c_scr, n_scr, m_scr, carry_scr, *, rest_off):
    head = pl.program_id(0)
    cb = CONV_CB
    d = stage.shape[1]
    steps_per_head = pl.num_programs(1) * pl.num_programs(2)
    t = pl.program_id(1) * pl.num_programs(2) + pl.program_id(2)
    slot = head % 2

    def window_copy(hd, g):
        row0 = pl.multiple_of(rest_off + g * d + hd * cb, SUBLANES)
        return pltpu.make_async_copy(wt_hbm.at[pl.ds(row0, cb), :], stage, sem)

    def store_window(dst_slot, g):
        step = 256
        for r in range(0, cb, step):
            w_scr[dst_slot, :, g * cb + r:g * cb + r + step] = stage[r:r + step, :].T.astype(w_scr.dtype)

    @pl.when((head == 0) & (t == 0))
    def _():
        for g in range(N_GROUPS):
            cp = window_copy(0, g)
            cp.start()
            cp.wait()
            store_window(0, g)

    g_next = t - (steps_per_head - N_GROUPS)
    prefetch = (g_next >= 0) & (head + 1 < pl.num_programs(0))

    @pl.when(prefetch)
    def _():
        window_copy(head + 1, g_next).start()

    @pl.when(pl.program_id(2) == 0)
    def _():
        c_scr[...] = jnp.zeros_like(c_scr)
        n_scr[...] = jnp.zeros_like(n_scr)
        m_scr[...] = jnp.zeros_like(m_scr)
        carry_scr[...] = jnp.zeros_like(carry_scr)

    L = MLSTM_L
    lane = lax.broadcasted_iota(jnp.int32, (L, LANES), 1)
    row = lax.broadcasted_iota(jnp.int32, (L, cb), 0)

    def head_col(rows, group):
        return jnp.sum(jnp.where(lane == head + group * HEADS, gcols_ref[0, rows, :], 0.0),
                       axis=-1, keepdims=True)

    def project(rows, half):
        cols = slice(half * 3 * cb, (half + 1) * 3 * cb)
        acc = jnp.dot(h_ref[0, rows, :], w_scr[slot, :, cols], preferred_element_type=jnp.float32)
        return acc + b_ref[0, :, cols]

    for cc in range(h_ref.shape[1] // L):
        rows = slice(cc * L, (cc + 1) * L)
        p1 = project(rows, 0)
        cu = p1[:, 2 * cb:3 * cb] * p1[:, 0:cb]
        prev = carry_scr[...]
        cu1 = jnp.where(row >= 1, pltpu.roll(cu, 1, 0), prev[7:8, :])
        cu2 = jnp.where(row >= 2, pltpu.roll(cu, 2, 0),
                        jnp.where(row == 1, prev[7:8, :], prev[6:7, :]))
        carry_scr[...] = cu[L - 8:, :]
        conv = cw_ref[0:1, :] * cu2 + cw_ref[1:2, :] * cu1 + cw_ref[2:3, :] * cu
        bconv = p1[:, cb:2 * cb] * conv

        out = _mlstm_chunk(q_ref[0, rows, :], k_ref[0, rows, :], v_ref[0, rows, :],
                           o_ref[0, rows, :], z_ref[0, rows, :],
                           head_col(rows, 0), head_col(rows, 1), head_col(rows, 2),
                           gat_ref[0, pl.ds(head, 1), rows], hw_ref[...], c_scr, n_scr, m_scr)
        ha_ref[0, rows, :] = out.astype(ha_ref.dtype)

        p2 = project(rows, 1)
        hb_ref[0, rows, :] = (bconv * _silu(p2[:, 0:cb])).astype(hb_ref.dtype)
        sga_ref[0, rows, :] = _sigmoid(p2[:, cb:2 * cb]).astype(sga_ref.dtype)
        sgb_ref[0, rows, :] = _sigmoid(p2[:, 2 * cb:3 * cb]).astype(sgb_ref.dtype)

    @pl.when(prefetch)
    def _():
        window_copy(head + 1, g_next).wait()

    for g in range(N_GROUPS):
        @pl.when(prefetch & (g_next == g))
        def _(g=g):
            store_window(1 - slot, g)


def _proj_b_call(h, wt, b_perm, conv_w, q_arr, proj_rest, g_cols, g_at, headnorm_w):
    bsz, s, d = h.shape
    tm = PROJ_TM
    cb = CONV_CB
    rest_off = MLSTM_WIDTH + 2 * HEADS
    assert rest_off % SUBLANES == 0 and cb % SUBLANES == 0 and d % SUBLANES == 0
    assert bsz * (s // tm) >= N_GROUPS

    qk_blk = lambda off: pl.BlockSpec((1, tm, QK_DIM), lambda hd, b, m: (b, m, off + hd))
    v_blk = lambda off: pl.BlockSpec((1, tm, V_DIM), lambda hd, b, m: (b, m, off + hd))
    out_blk = pl.BlockSpec((1, tm, cb), lambda hd, b, m: (b, m, hd))
    out_sds = jax.ShapeDtypeStruct((bsz, s, d), jnp.bfloat16)
    return pl.pallas_call(
        functools.partial(_proj_b_kernel, rest_off=rest_off),
        grid=(HEADS, bsz, s // tm),
        in_specs=[
            pl.BlockSpec((1, tm, d), lambda hd, b, m: (b, m, 0)),
            pl.BlockSpec(memory_space=pl.ANY),
            pl.BlockSpec((1, 1, N_GROUPS * cb), lambda hd, b, m: (hd, 0, 0)),
            pl.BlockSpec((CONV_K, cb), lambda hd, b, m: (0, hd)),
            qk_blk(0), qk_blk(0),
            v_blk(QK_WIDTH // V_DIM), v_blk((QK_WIDTH + V_WIDTH) // V_DIM),
            v_blk((QK_WIDTH + 2 * V_WIDTH) // V_DIM),
            pl.BlockSpec((1, tm, LANES), lambda hd, b, m: (b, m, 0)),
            pl.BlockSpec((1, HEADS, tm), lambda hd, b, m: (b, 0, m)),
            pl.BlockSpec((1, V_DIM), lambda hd, b, m: (0, hd)),
        ],
        out_specs=[out_blk, out_blk, out_blk, out_blk],
        out_shape=[out_sds, out_sds, out_sds, out_sds],
        scratch_shapes=[
            pltpu.VMEM((2, d, N_GROUPS * cb), jnp.bfloat16),
            pltpu.VMEM((cb, d), jnp.float32),
            pltpu.SemaphoreType.DMA(()),
            pltpu.VMEM((QK_DIM, V_DIM), jnp.float32),
            pltpu.VMEM((1, QK_DIM), jnp.float32),
            pltpu.VMEM((1, 1), jnp.float32),
            pltpu.VMEM((8, cb), jnp.float32),
        ],
        compiler_params=pltpu.CompilerParams(
            dimension_semantics=("arbitrary", "arbitrary", "arbitrary"),
            vmem_limit_bytes=VMEM_LIMIT),
        name="proj_conv_mlstm",
    )(h, wt, b_perm, conv_w,
      q_arr, proj_rest, proj_rest, proj_rest, proj_rest, g_cols, g_at, headnorm_w)


def _merge_kernel(ha_ref, hb_ref, sga_ref, sgb_ref, x_ref, gate_ref,
                  wa_ref, wb_ref, wo_ref, nf_ref, out_ref, *, final_norm):
    y_a = jnp.dot(ha_ref[0], wa_ref[...], preferred_element_type=jnp.float32)
    y_b = jnp.dot(hb_ref[0], wb_ref[...], preferred_element_type=jnp.float32)
    merged = sga_ref[0].astype(jnp.float32) * y_a + sgb_ref[0].astype(jnp.float32) * y_b
    o = jnp.dot(merged.astype(jnp.bfloat16), wo_ref[...], preferred_element_type=jnp.float32)
    xo = x_ref[0] + gate_ref[0] * o
    if final_norm:
        ms = jnp.mean(xo * xo, axis=-1, keepdims=True)
        xo = xo * lax.rsqrt(ms + EPS) * nf_ref[...]
    out_ref[0] = xo


def _merge_call(h_a, h_b, sg_a, sg_b, x, gate, w_proj_a, w_proj_b, w_out, normf_w, final_norm):
    bsz, s, d = x.shape
    tm = MERGE_TM
    tile = pl.BlockSpec((1, tm, d), lambda b, t: (b, t, 0))
    const = lambda shape: pl.BlockSpec(shape, lambda b, t: (0,) * len(shape),
                                       pipeline_mode=pl.Buffered(1))
    return pl.pallas_call(
        functools.partial(_merge_kernel, final_norm=final_norm),
        grid=(bsz, s // tm),
        in_specs=[
            tile, tile, tile, tile,
            tile,
            pl.BlockSpec((1, 1, d), lambda b, t: (b, 0, 0)),
            const((V_WIDTH, d)), const((d, d)), const((d, d)),
            const((1, d)),
        ],
        out_specs=tile,
        out_shape=jax.ShapeDtypeStruct((bsz, s, d), jnp.float32),
        compiler_params=pltpu.CompilerParams(
            dimension_semantics=("arbitrary", "arbitrary"), vmem_limit_bytes=VMEM_LIMIT),
        name="merge_out",
    )(h_a, h_b, sg_a, sg_b, x, gate, w_proj_a, w_proj_b, w_out, normf_w)


def kernel(x, c, norm1_w, w_ada, b_ada, w_in, b_in, conv_w, headnorm_w,
           w_proj_a, w_proj_b, w_out, normf_w):
    bsz, s, d = x.shape
    depth = norm1_w.shape[0]
    gate_off = MLSTM_WIDTH
    rest_off = gate_off + 2 * HEADS
    bf16 = jnp.bfloat16

    c_pad = jnp.pad(c, ((0, 8 - bsz), (0, 0)))
    for l in range(depth):
        mod = _ada_call(c_pad, w_ada[l], b_ada[l][None, :])[:bsz]
        shift, scale, gate = (m[:, None, :] for m in jnp.split(mod, 3, axis=-1))

        wt, b_l = jnp.swapaxes(w_in[l], 0, 1), b_in[l]
        gate_pad = LANES - 2 * HEADS
        assert w_proj_a[l].shape == w_proj_b[l].shape == w_out[l].shape == (d, d)
        h, g_cols, g_at, q_arr, wa16, wb16, wo16 = _norm_q_call(
            x, scale, shift, norm1_w[l][None, :],
            jnp.pad(wt[gate_off:rest_off], ((0, gate_pad), (0, 0))),
            jnp.pad(b_l[None, gate_off:rest_off], ((0, 0), (0, gate_pad))),
            wt, b_l[None, :QK_WIDTH], w_proj_a[l], w_proj_b[l], w_out[l])
        proj_rest = _proj_a_call(h, wt, b_l[None, QK_WIDTH:gate_off])
        b_perm = b_l[rest_off:].reshape(6, d // CONV_CB, CONV_CB).transpose(1, 0, 2)
        b_perm = b_perm.reshape(d // CONV_CB, 1, 6 * CONV_CB)
        h_a, h_b, sg_a, sg_b = _proj_b_call(h, wt, b_perm, conv_w[l], q_arr, proj_rest, g_cols, g_at,
                                            headnorm_w[l][None, :])
        x = _merge_call(h_a, h_b, sg_a, sg_b, x, gate, wa16, wb16, wo16, normf_w[None, :],
                        final_norm=(l == depth - 1))
    return x
```

```python
import functools
import math

import jax
import jax.numpy as jnp
from jax import lax
from jax.experimental import pallas as pl
from jax.experimental.pallas import tpu as pltpu

HEADS = 8
QK_DIM = 128
V_DIM = 256
QK_WIDTH = HEADS * QK_DIM
V_WIDTH = HEADS * V_DIM
MLSTM_WIDTH = 2 * QK_WIDTH + 3 * V_WIDTH
CONV_K = 3
EPS = 1e-6

LANES = 128
SUBLANES = 8

ADA_TN = 768
NORM_TM = 512
PROJ_TM = 2048
PROJA_TM = 2048
PROJA_TN = 1024
MLSTM_L = 256
MERGE_TM = 256
CONV_CB = V_DIM

VMEM_LIMIT = 58 * 1024 * 1024


def _sigmoid(v):
    return 0.5 * jnp.tanh(0.5 * v) + 0.5


def _silu(v):
    return v * _sigmoid(v)


def _ada_kernel(c_ref, w_ref, b_ref, o_ref):
    c_act = _silu(c_ref[...])
    o_ref[...] = jnp.dot(c_act, w_ref[...], preferred_element_type=jnp.float32) + b_ref[...]


def _ada_call(c_pad, w_ada, b_ada):
    rows, d = c_pad.shape
    n = w_ada.shape[1]
    return pl.pallas_call(
        _ada_kernel,
        grid=(n // ADA_TN,),
        in_specs=[
            pl.BlockSpec((rows, d), lambda j: (0, 0)),
            pl.BlockSpec((d, ADA_TN), lambda j: (0, j)),
            pl.BlockSpec((1, ADA_TN), lambda j: (0, j)),
        ],
        out_specs=pl.BlockSpec((rows, ADA_TN), lambda j: (0, j)),
        out_shape=jax.ShapeDtypeStruct((rows, n), jnp.float32),
        compiler_params=pltpu.CompilerParams(
            dimension_semantics=("arbitrary",), vmem_limit_bytes=VMEM_LIMIT),
        name="adaln_mod",
    )(c_pad, w_ada, b_ada)


def _chunk_scan_rows(v, combine, fill, chunk):
    pos = lax.broadcasted_iota(jnp.int32, v.shape, 0) % chunk
    step = 1
    while step < chunk:
        shifted = pltpu.roll(v, step, 0)
        v = combine(v, jnp.where(pos >= step, shifted, fill))
        step *= 2
    return v


def _load_weight_block(wt_ref, w_scr, col0):
    step = 256
    for r in range(0, wt_ref.shape[0], step):
        w_scr[:, col0 + r:col0 + r + step] = wt_ref[r:r + step, :].T.astype(w_scr.dtype)


def _norm_q_kernel(x_ref, scale_ref, shift_ref, nw_ref, wg_ref, bg_ref, wt_ref, bq_ref,
                   wa_ref, wb_ref, wo_ref,
                   h_ref, cols_ref, at_ref, q_ref, wa16_ref, wb16_ref, wo16_ref,
                   w_scr, h_scr):
    s = pl.program_id(0)
    slot = s % 2
    half = w_scr.shape[1] // 2

    @pl.when(s == 0)
    def _():
        _load_weight_block(wt_ref, w_scr, 0)
        h_scr[1] = jnp.zeros(h_scr.shape[1:], h_scr.dtype)

    h_prev = h_scr[1 - slot]
    q_ref[0, :, :half] = (jnp.dot(h_prev, w_scr[:, :half], preferred_element_type=jnp.float32)
                          + bq_ref[:, :half]).astype(q_ref.dtype)

    xf = x_ref[0]
    ms = jnp.mean(xf * xf, axis=-1, keepdims=True)
    y = xf * lax.rsqrt(ms + EPS) * nw_ref[...]
    h = (y * (1.0 + scale_ref[0]) + shift_ref[0]).astype(jnp.bfloat16)
    h_ref[0] = h
    h_scr[slot] = h
    g = lax.dot_general(h, wg_ref[...].astype(jnp.bfloat16), (((1,), (1,)), ((), ())),
                        preferred_element_type=jnp.float32) + bg_ref[...]
    lf = jax.nn.log_sigmoid(pltpu.roll(g, LANES - HEADS, 1))
    b = _chunk_scan_rows(lf, jnp.add, 0.0, MLSTM_L)
    a = g - b
    cm = _chunk_scan_rows(a, jnp.maximum, -jnp.inf, MLSTM_L)
    lane = lax.broadcasted_iota(jnp.int32, g.shape, 1)
    cols_ref[0] = jnp.where(lane < HEADS, b,
                            jnp.where(lane < 2 * HEADS, pltpu.roll(a, HEADS, 1),
                                      pltpu.roll(cm, 2 * HEADS, 1)))
    at_ref[0] = a.T[:HEADS, :]

    q_ref[0, :, half:] = (jnp.dot(h_prev, w_scr[:, half:], preferred_element_type=jnp.float32)
                          + bq_ref[:, half:]).astype(q_ref.dtype)
    wa16_ref[...] = wa_ref[...].astype(wa16_ref.dtype)
    wb16_ref[...] = wb_ref[...].astype(wb16_ref.dtype)
    wo16_ref[...] = wo_ref[...].astype(wo16_ref.dtype)


def _norm_q_call(x, scale, shift, norm_w, wt_gate, b_gate, wt, b_q, w_proj_a, w_proj_b, w_out):
    bsz, s, d = x.shape
    tm, tn = NORM_TM, 2 * QK_WIDTH
    nm = s // tm
    tiles = bsz * nm
    slab = d // tiles
    assert slab * tiles == d and slab % 16 == 0
    cur = lambda i: jnp.minimum(i, tiles - 1)
    prv = lambda i: jnp.maximum(i - 1, 0)
    row_blk = lambda width, t: pl.BlockSpec((1, tm, width), lambda i: (t(i) // nm, t(i) % nm, 0))
    mod_blk = pl.BlockSpec((1, 1, d), lambda i: (cur(i) // nm, 0, 0))
    const = lambda shape: pl.BlockSpec(shape, lambda i: (0,) * len(shape), pipeline_mode=pl.Buffered(1))
    slab_blk = pl.BlockSpec((slab, d), lambda i: (cur(i), 0))
    w16_sds = jax.ShapeDtypeStruct((d, d), jnp.bfloat16)
    return pl.pallas_call(
        _norm_q_kernel,
        grid=(tiles + 1,),
        in_specs=[
            row_blk(d, cur), mod_blk, mod_blk, const((1, d)),
            const((LANES, d)), const((1, LANES)),
            const((tn, d)), const((1, tn)),
            slab_blk, slab_blk, slab_blk,
        ],
        out_specs=[
            row_blk(d, cur), row_blk(LANES, cur),
            pl.BlockSpec((1, HEADS, tm), lambda i: (cur(i) // nm, 0, cur(i) % nm)),
            row_blk(tn, prv),
            slab_blk, slab_blk, slab_blk,
        ],
        out_shape=[
            jax.ShapeDtypeStruct((bsz, s, d), jnp.bfloat16),
            jax.ShapeDtypeStruct((bsz, s, LANES), jnp.float32),
            jax.ShapeDtypeStruct((bsz, HEADS, s), jnp.float32),
            jax.ShapeDtypeStruct((bsz, s, tn), jnp.bfloat16),
            w16_sds, w16_sds, w16_sds,
        ],
        scratch_shapes=[pltpu.VMEM((d, tn), jnp.bfloat16), pltpu.VMEM((2, tm, d), jnp.bfloat16)],
        compiler_params=pltpu.CompilerParams(
            dimension_semantics=("arbitrary",), vmem_limit_bytes=VMEM_LIMIT),
        name="modnorm_q",
    )(x, scale, shift, norm_w, wt_gate, b_gate, wt, b_q, w_proj_a, w_proj_b, w_out)


def _proj_a_kernel(h_ref, wt_ref, b_ref, o_ref, w_scr):
    @pl.when((pl.program_id(1) == 0) & (pl.program_id(2) == 0))
    def _():
        _load_weight_block(wt_ref, w_scr, 0)

    acc = jnp.dot(h_ref[0], w_scr[...], preferred_element_type=jnp.float32)
    o_ref[0] = (acc + b_ref[...]).astype(o_ref.dtype)


def _proj_a_call(h, wt, b_rest):
    bsz, s, d = h.shape
    n = MLSTM_WIDTH - 2 * QK_WIDTH
    tm, tn = PROJA_TM, PROJA_TN
    first = 2 * QK_WIDTH // tn
    return pl.pallas_call(
        _proj_a_kernel,
        grid=(n // tn, bsz, s // tm),
        in_specs=[
            pl.BlockSpec((1, tm, d), lambda j, b, m: (b, m, 0)),
            pl.BlockSpec((tn, d), lambda j, b, m: (j + first, 0)),
            pl.BlockSpec((1, tn), lambda j, b, m: (0, j)),
        ],
        out_specs=pl.BlockSpec((1, tm, tn), lambda j, b, m: (b, m, j)),
        out_shape=jax.ShapeDtypeStruct((bsz, s, n), jnp.bfloat16),
        scratch_shapes=[pltpu.VMEM((d, tn), jnp.bfloat16)],
        compiler_params=pltpu.CompilerParams(
            dimension_semantics=("arbitrary", "arbitrary", "arbitrary"),
            vmem_limit_bytes=VMEM_LIMIT),
        name="proj_mlstm",
    )(h, wt, b_rest)


def _mlstm_chunk(qh, kh, vh, og, zg, b, a, cm, a_row, hw, c_scr, n_scr, m_scr):
    L = qh.shape[0]
    m_state = m_scr[...]
    big_m = jnp.maximum(m_state, cm)
    w_inter = jnp.exp(m_state - big_m) * (QK_DIM ** -0.5)
    clamp = jnp.exp(-b - big_m)
    m_last = big_m[L - 1:L, :]
    ws = jnp.exp(a - m_last)
    keep = jnp.exp(m_state - m_last)
    m_scr[...] = b[L - 1:L, :] + m_last
    neg_m = math.log(QK_DIM ** -0.5) - big_m

    ri = lax.broadcasted_iota(jnp.int32, (L, L), 0)
    ci = lax.broadcasted_iota(jnp.int32, (L, L), 1)
    c_h = c_scr[...]
    n_h = n_scr[...]

    qk = lax.dot_general(qh, kh, (((1,), (1,)), ((), ())), preferred_element_type=jnp.float32)
    scores = jnp.where(ci <= ri, jnp.exp(a_row + neg_m), 0.0) * qk
    num = jnp.dot(scores.astype(jnp.bfloat16), vh, preferred_element_type=jnp.float32)
    num = num + w_inter * jnp.dot(qh, c_h.astype(jnp.bfloat16), preferred_element_type=jnp.float32)
    den = jnp.sum(scores, axis=-1, keepdims=True) \
        + w_inter * jnp.sum(qh.astype(jnp.float32) * n_h, axis=-1, keepdims=True)
    h_out = num * (1.0 / jnp.maximum(jnp.abs(den), clamp))

    vw = (ws * vh.astype(jnp.float32)).astype(jnp.bfloat16)
    kv = lax.dot_general(kh, vw, (((0,), (0,)), ((), ())), preferred_element_type=jnp.float32)
    c_scr[...] = keep * c_h + kv
    n_scr[...] = keep * n_h + jnp.sum(ws * kh.astype(jnp.float32), axis=0, keepdims=True)

    hg = _sigmoid(og.astype(jnp.float32)) * h_out
    ms = jnp.mean(hg * hg, axis=-1, keepdims=True)
    return hg * lax.rsqrt(ms + EPS) * hw * _silu(zg.astype(jnp.float32))


N_GROUPS = 6


def _proj_b_kernel(h_ref, wt_hbm, b_ref, cw_ref,
                   q_ref, k_ref, v_ref, o_ref, z_ref, gcols_ref, gat_ref, hw_ref,
                   ha_ref, hb_ref, sga_ref, sgb_ref,
                   w_scr, stage, sem, c_scr, n_scr, m_scr, carry_scr, *, rest_off):
    head = pl.program_id(0)
    cb = CONV_CB
    d = stage.shape[1]
    steps_per_head = pl.num_programs(1) * pl.num_programs(2)
    t = pl.program_id(1) * pl.num_programs(2) + pl.program_id(2)
    slot = head % 2

    def window_copy(hd, g):
        row0 = pl.multiple_of(rest_off + g * d + hd * cb, SUBLANES)
        return pltpu.make_async_copy(wt_hbm.at[pl.ds(row0, cb), :], stage, sem)

    def store_window(dst_slot, g):
        step = 256
        for r in range(0, cb, step):
            w_scr[dst_slot, :, g * cb + r:g * cb + r + step] = stage[r:r + step, :].T.astype(w_scr.dtype)

    @pl.when((head == 0) & (t == 0))
    def _():
        for g in range(N_GROUPS):
            cp = window_copy(0, g)
            cp.start()
            cp.wait()
            store_window(0, g)

    g_next = t - (steps_per_head - N_GROUPS)
    prefetch = (g_next >= 0) & (head + 1 < pl.num_programs(0))

    @pl.when(prefetch)
    def _():
        window_copy(head + 1, g_next).start()

    @pl.when(pl.program_id(2) == 0)
    def _():
        c_scr[...] = jnp.zeros_like(c_scr)
        n_scr[...] = jnp.zeros_like(n_scr)
        m_scr[...] = jnp.zeros_like(m_scr)
        carry_scr[...] = jnp.zeros_like(carry_scr)

    L = MLSTM_L
    lane = lax.broadcasted_iota(jnp.int32, (L, LANES), 1)
    row = lax.broadcasted_iota(jnp.int32, (L, cb), 0)

    def head_col(rows, group):
        return jnp.sum(jnp.where(lane == head + group * HEADS, gcols_ref[0, rows, :], 0.0),
                       axis=-1, keepdims=True)

    def project(rows, half):
        cols = slice(half * 3 * cb, (half + 1) * 3 * cb)
        acc = jnp.dot(h_ref[0, rows, :], w_scr[slot, :, cols], preferred_element_type=jnp.float32)
        return acc + b_ref[0, :, cols]

    for cc in range(h_ref.shape[1] // L):
        rows = slice(cc * L, (cc + 1) * L)
        p1 = project(rows, 0)
        cu = p1[:, 2 * cb:3 * cb] * p1[:, 0:cb]
        prev = carry_scr[...]
        cu1 = jnp.where(row >= 1, pltpu.roll(cu, 1, 0), prev[7:8, :])
        cu2 = jnp.where(row >= 2, pltpu.roll(cu, 2, 0),
                        jnp.where(row == 1, prev[7:8, :], prev[6:7, :]))
        carry_scr[...] = cu[L - 8:, :]
        conv = cw_ref[0:1, :] * cu2 + cw_ref[1:2, :] * cu1 + cw_ref[2:3, :] * cu
        bconv = p1[:, cb:2 * cb] * conv

        out = _mlstm_chunk(q_ref[0, rows, :], k_ref[0, rows, :], v_ref[0, rows, :],
                           o_ref[0, rows, :], z_ref[0, rows, :],
                           head_col(rows, 0), head_col(rows, 1), head_col(rows, 2),
                           gat_ref[0, pl.ds(head, 1), rows], hw_ref[...], c_scr, n_scr, m_scr)
        ha_ref[0, rows, :] = out.astype(ha_ref.dtype)

        p2 = project(rows, 1)
        hb_ref[0, rows, :] = (bconv * _silu(p2[:, 0:cb])).astype(hb_ref.dtype)
        sga_ref[0, rows, :] = _sigmoid(p2[:, cb:2 * cb]).astype(sga_ref.dtype)
        sgb_ref[0, rows, :] = _sigmoid(p2[:, 2 * cb:3 * cb]).astype(sgb_ref.dtype)

    @pl.when(prefetch)
    def _():
        window_copy(head + 1, g_next).wait()

    for g in range(N_GROUPS):
        @pl.when(prefetch & (g_next == g))
        def _(g=g):
            store_window(1 - slot, g)


def _proj_b_call(h, wt, b_perm, conv_w, q_arr, proj_rest, g_cols, g_at, headnorm_w):
    bsz, s, d = h.shape
    tm = PROJ_TM
    cb = CONV_CB
    rest_off = MLSTM_WIDTH + 2 * HEADS
    assert rest_off % SUBLANES == 0 and cb % SUBLANES == 0 and d % SUBLANES == 0
    assert bsz * (s // tm) >= N_GROUPS

    qk_blk = lambda off: pl.BlockSpec((1, tm, QK_DIM), lambda hd, b, m: (b, m, off + hd))
    v_blk = lambda off: pl.BlockSpec((1, tm, V_DIM), lambda hd, b, m: (b, m, off + hd))
    out_blk = pl.BlockSpec((1, tm, cb), lambda hd, b, m: (b, m, hd))
    out_sds = jax.ShapeDtypeStruct((bsz, s, d), jnp.bfloat16)
    return pl.pallas_call(
        functools.partial(_proj_b_kernel, rest_off=rest_off),
        grid=(HEADS, bsz, s // tm),
        in_specs=[
            pl.BlockSpec((1, tm, d), lambda hd, b, m: (b, m, 0)),
            pl.BlockSpec(memory_space=pl.ANY),
            pl.BlockSpec((1, 1, N_GROUPS * cb), lambda hd, b, m: (hd, 0, 0)),
            pl.BlockSpec((CONV_K, cb), lambda hd, b, m: (0, hd)),
            qk_blk(0), qk_blk(HEADS),
            v_blk(0), v_blk(HEADS), v_blk(2 * HEADS),
            pl.BlockSpec((1, tm, LANES), lambda hd, b, m: (b, m, 0)),
            pl.BlockSpec((1, HEADS, tm), lambda hd, b, m: (b, 0, m)),
            pl.BlockSpec((1, V_DIM), lambda hd, b, m: (0, hd)),
        ],
        out_specs=[out_blk, out_blk, out_blk, out_blk],
        out_shape=[out_sds, out_sds, out_sds, out_sds],
        scratch_shapes=[
            pltpu.VMEM((2, d, N_GROUPS * cb), jnp.bfloat16),
            pltpu.VMEM((cb, d), jnp.float32),
            pltpu.SemaphoreType.DMA(()),
            pltpu.VMEM((QK_DIM, V_DIM), jnp.float32),
            pltpu.VMEM((1, QK_DIM), jnp.float32),
            pltpu.VMEM((1, 1), jnp.float32),
            pltpu.VMEM((8, cb), jnp.float32),
        ],
        compiler_params=pltpu.CompilerParams(
            dimension_semantics=("arbitrary", "arbitrary", "arbitrary"),
            vmem_limit_bytes=VMEM_LIMIT),
        name="proj_conv_mlstm",
    )(h, wt, b_perm, conv_w,
      q_arr, q_arr, proj_rest, proj_rest, proj_rest, g_cols, g_at, headnorm_w)


def _merge_kernel(ha_ref, hb_ref, sga_ref, sgb_ref, x_ref, gate_ref,
                  wa_ref, wb_ref, wo_ref, nf_ref, out_ref, *, final_norm):
    y_a = jnp.dot(ha_ref[0], wa_ref[...], preferred_element_type=jnp.float32)
    y_b = jnp.dot(hb_ref[0], wb_ref[...], preferred_element_type=jnp.float32)
    merged = sga_ref[0].astype(jnp.float32) * y_a + sgb_ref[0].astype(jnp.float32) * y_b
    o = jnp.dot(merged.astype(jnp.bfloat16), wo_ref[...], preferred_element_type=jnp.float32)
    xo = x_ref[0] + gate_ref[0] * o
    if final_norm:
        ms = jnp.mean(xo * xo, axis=-1, keepdims=True)
        xo = xo * lax.rsqrt(ms + EPS) * nf_ref[...]
    out_ref[0] = xo


def _merge_call(h_a, h_b, sg_a, sg_b, x, gate, w_proj_a, w_proj_b, w_out, normf_w, final_norm):
    bsz, s, d = x.shape
    tm = MERGE_TM
    tile = pl.BlockSpec((1, tm, d), lambda b, t: (b, t, 0))
    const = lambda shape: pl.BlockSpec(shape, lambda b, t: (0,) * len(shape),
                                       pipeline_mode=pl.Buffered(1))
    return pl.pallas_call(
        functools.partial(_merge_kernel, final_norm=final_norm),
        grid=(bsz, s // tm),
        in_specs=[
            tile, tile, tile, tile,
            tile,
            pl.BlockSpec((1, 1, d), lambda b, t: (b, 0, 0)),
            const((V_WIDTH, d)), const((d, d)), const((d, d)),
            const((1, d)),
        ],
        out_specs=tile,
        out_shape=jax.ShapeDtypeStruct((bsz, s, d), jnp.float32),
        compiler_params=pltpu.CompilerParams(
            dimension_semantics=("arbitrary", "arbitrary"), vmem_limit_bytes=VMEM_LIMIT),
        name="merge_out",
    )(h_a, h_b, sg_a, sg_b, x, gate, w_proj_a, w_proj_b, w_out, normf_w)


def kernel(x, c, norm1_w, w_ada, b_ada, w_in, b_in, conv_w, headnorm_w,
           w_proj_a, w_proj_b, w_out, normf_w):
    bsz, s, d = x.shape
    depth = norm1_w.shape[0]
    gate_off = MLSTM_WIDTH
    rest_off = gate_off + 2 * HEADS
    bf16 = jnp.bfloat16

    c_pad = jnp.pad(c, ((0, 8 - bsz), (0, 0)))
    for l in range(depth):
        mod = _ada_call(c_pad, w_ada[l], b_ada[l][None, :])[:bsz]
        shift, scale, gate = (m[:, None, :] for m in jnp.split(mod, 3, axis=-1))

        wt, b_l = jnp.swapaxes(w_in[l], 0, 1), b_in[l]
        gate_pad = LANES - 2 * HEADS
        assert w_proj_a[l].shape == w_proj_b[l].shape == w_out[l].shape == (d, d)
        h, g_cols, g_at, q_arr, wa16, wb16, wo16 = _norm_q_call(
            x, scale, shift, norm1_w[l][None, :],
            jnp.pad(wt[gate_off:rest_off], ((0, gate_pad), (0, 0))),
            jnp.pad(b_l[None, gate_off:rest_off], ((0, 0), (0, gate_pad))),
            wt, b_l[None, :2 * QK_WIDTH], w_proj_a[l], w_proj_b[l], w_out[l])
        proj_rest = _proj_a_call(h, wt, b_l[None, 2 * QK_WIDTH:gate_off])
        b_perm = b_l[rest_off:].reshape(6, d // CONV_CB, CONV_CB).transpose(1, 0, 2)
        b_perm = b_perm.reshape(d // CONV_CB, 1, 6 * CONV_CB)
        h_a, h_b, sg_a, sg_b = _proj_b_call(h, wt, b_perm, conv_w[l], q_arr, proj_rest, g_cols, g_at,
                                            headnorm_w[l][None, :])
        x = _merge_call(h_a, h_b, sg_a, sg_b, x, gate, wa16, wb16, wo16, normf_w[None, :],
                        final_norm=(l == depth - 1))
    return x
```

```python
import functools
import math

import jax
import jax.numpy as jnp
from jax import lax
from jax.experimental import pallas as pl
from jax.experimental.pallas import tpu as pltpu

HEADS = 8
QK_DIM = 128
V_DIM = 256
QK_WIDTH = HEADS * QK_DIM
V_WIDTH = HEADS * V_DIM
MLSTM_WIDTH = 2 * QK_WIDTH + 3 * V_WIDTH
CONV_K = 3
EPS = 1e-6

LANES = 128
SUBLANES = 8

ADA_TN = 1536
NORM_TM = 512
PROJ_TM = 2048
PROJA_TM = 2048
PROJA_TN = 1024
MLSTM_L = 256
MERGE_TM = 256
CONV_CB = V_DIM

VMEM_LIMIT = 58 * 1024 * 1024


def _sigmoid(v):
    return 0.5 * jnp.tanh(0.5 * v) + 0.5


def _silu(v):
    return v * _sigmoid(v)


def _ada_kernel(c_ref, w_ref, b_ref, o_ref):
    c_act = _silu(c_ref[...])
    o_ref[...] = jnp.dot(c_act, w_ref[...], preferred_element_type=jnp.float32) + b_ref[...]


def _ada_call(c_pad, w_ada, b_ada):
    rows, d = c_pad.shape
    n = w_ada.shape[1]
    return pl.pallas_call(
        _ada_kernel,
        grid=(n // ADA_TN,),
        in_specs=[
            pl.BlockSpec((rows, d), lambda j: (0, 0)),
            pl.BlockSpec((d, ADA_TN), lambda j: (0, j)),
            pl.BlockSpec((1, ADA_TN), lambda j: (0, j)),
        ],
        out_specs=pl.BlockSpec((rows, ADA_TN), lambda j: (0, j)),
        out_shape=jax.ShapeDtypeStruct((rows, n), jnp.float32),
        compiler_params=pltpu.CompilerParams(
            dimension_semantics=("arbitrary",), vmem_limit_bytes=VMEM_LIMIT),
        name="adaln_mod",
    )(c_pad, w_ada, b_ada)


def _chunk_scan_rows(v, combine, fill, chunk):
    pos = lax.broadcasted_iota(jnp.int32, v.shape, 0) % chunk
    step = 1
    while step < chunk:
        shifted = pltpu.roll(v, step, 0)
        v = combine(v, jnp.where(pos >= step, shifted, fill))
        step *= 2
    return v


def _load_weight_block(wt_ref, w_scr, col0):
    step = 256
    for r in range(0, wt_ref.shape[0], step):
        w_scr[:, col0 + r:col0 + r + step] = wt_ref[r:r + step, :].T.astype(w_scr.dtype)


def _norm_q_kernel(x_ref, scale_ref, shift_ref, nw_ref, wg_ref, bg_ref, wt_ref, bq_ref,
                   wa_ref, wb_ref, wo_ref,
                   h_ref, cols_ref, at_ref, q_ref, wa16_ref, wb16_ref, wo16_ref,
                   w_scr, h_scr):
    s = pl.program_id(0)
    slot = s % 2
    half = w_scr.shape[1] // 2

    @pl.when(s == 0)
    def _():
        _load_weight_block(wt_ref, w_scr, 0)
        h_scr[1] = jnp.zeros(h_scr.shape[1:], h_scr.dtype)

    h_prev = h_scr[1 - slot]
    q_ref[0, :, :half] = (jnp.dot(h_prev, w_scr[:, :half], preferred_element_type=jnp.float32)
                          + bq_ref[:, :half]).astype(q_ref.dtype)

    xf = x_ref[0]
    ms = jnp.mean(xf * xf, axis=-1, keepdims=True)
    y = xf * lax.rsqrt(ms + EPS) * nw_ref[...]
    h = (y * (1.0 + scale_ref[0]) + shift_ref[0]).astype(jnp.bfloat16)
    h_ref[0] = h
    h_scr[slot] = h
    g = lax.dot_general(h, wg_ref[...].astype(jnp.bfloat16), (((1,), (1,)), ((), ())),
                        preferred_element_type=jnp.float32) + bg_ref[...]
    lf = jax.nn.log_sigmoid(pltpu.roll(g, LANES - HEADS, 1))
    b = _chunk_scan_rows(lf, jnp.add, 0.0, MLSTM_L)
    a = g - b
    cm = _chunk_scan_rows(a, jnp.maximum, -jnp.inf, MLSTM_L)
    lane = lax.broadcasted_iota(jnp.int32, g.shape, 1)
    cols_ref[0] = jnp.where(lane < HEADS, b,
                            jnp.where(lane < 2 * HEADS, pltpu.roll(a, HEADS, 1),
                                      pltpu.roll(cm, 2 * HEADS, 1)))
    at_ref[0] = a.T[:HEADS, :]

    q_ref[0, :, half:] = (jnp.dot(h_prev, w_scr[:, half:], preferred_element_type=jnp.float32)
                          + bq_ref[:, half:]).astype(q_ref.dtype)
    wa16_ref[...] = wa_ref[...].astype(wa16_ref.dtype)
    wb16_ref[...] = wb_ref[...].astype(wb16_ref.dtype)
    wo16_ref[...] = wo_ref[...].astype(wo16_ref.dtype)


def _norm_q_call(x, scale, shift, norm_w, wt_gate, b_gate, wt, b_q, w_proj_a, w_proj_b, w_out):
    bsz, s, d = x.shape
    tm, tn = NORM_TM, 2 * QK_WIDTH
    nm = s // tm
    tiles = bsz * nm
    slab = d // tiles
    assert slab * tiles == d and slab % 16 == 0
    cur = lambda i: jnp.minimum(i, tiles - 1)
    prv = lambda i: jnp.maximum(i - 1, 0)
    row_blk = lambda width, t: pl.BlockSpec((1, tm, width), lambda i: (t(i) // nm, t(i) % nm, 0))
    mod_blk = pl.BlockSpec((1, 1, d), lambda i: (cur(i) // nm, 0, 0))
    const = lambda shape: pl.BlockSpec(shape, lambda i: (0,) * len(shape), pipeline_mode=pl.Buffered(1))
    slab_blk = pl.BlockSpec((slab, d), lambda i: (cur(i), 0))
    w16_sds = jax.ShapeDtypeStruct((d, d), jnp.bfloat16)
    return pl.pallas_call(
        _norm_q_kernel,
        grid=(tiles + 1,),
        in_specs=[
            row_blk(d, cur), mod_blk, mod_blk, const((1, d)),
            const((LANES, d)), const((1, LANES)),
            const((tn, d)), const((1, tn)),
            slab_blk, slab_blk, slab_blk,
        ],
        out_specs=[
            row_blk(d, cur), row_blk(LANES, cur),
            pl.BlockSpec((1, HEADS, tm), lambda i: (cur(i) // nm, 0, cur(i) % nm)),
            row_blk(tn, prv),
            slab_blk, slab_blk, slab_blk,
        ],
        out_shape=[
            jax.ShapeDtypeStruct((bsz, s, d), jnp.bfloat16),
            jax.ShapeDtypeStruct((bsz, s, LANES), jnp.float32),
            jax.ShapeDtypeStruct((bsz, HEADS, s), jnp.float32),
            jax.ShapeDtypeStruct((bsz, s, tn), jnp.bfloat16),
            w16_sds, w16_sds, w16_sds,
        ],
        scratch_shapes=[pltpu.VMEM((d, tn), jnp.bfloat16), pltpu.VMEM((2, tm, d), jnp.bfloat16)],
        compiler_params=pltpu.CompilerParams(
            dimension_semantics=("arbitrary",), vmem_limit_bytes=VMEM_LIMIT),
        name="modnorm_q",
    )(x, scale, shift, norm_w, wt_gate, b_gate, wt, b_q, w_proj_a, w_proj_b, w_out)


def _proj_a_kernel(h_ref, wt_ref, b_ref, o_ref, w_scr):
    @pl.when((pl.program_id(1) == 0) & (pl.program_id(2) == 0))
    def _():
        _load_weight_block(wt_ref, w_scr, 0)

    acc = jnp.dot(h_ref[0], w_scr[...], preferred_element_type=jnp.float32)
    o_ref[0] = (acc + b_ref[...]).astype(o_ref.dtype)


def _proj_a_call(h, wt, b_rest):
    bsz, s, d = h.shape
    n = MLSTM_WIDTH - 2 * QK_WIDTH
    tm, tn = PROJA_TM, PROJA_TN
    first = 2 * QK_WIDTH // tn
    return pl.pallas_call(
        _proj_a_kernel,
        grid=(n // tn, bsz, s // tm),
        in_specs=[
            pl.BlockSpec((1, tm, d), lambda j, b, m: (b, m, 0)),
            pl.BlockSpec((tn, d), lambda j, b, m: (j + first, 0)),
            pl.BlockSpec((1, tn), lambda j, b, m: (0, j)),
        ],
        out_specs=pl.BlockSpec((1, tm, tn), lambda j, b, m: (b, m, j)),
        out_shape=jax.ShapeDtypeStruct((bsz, s, n), jnp.bfloat16),
        scratch_shapes=[pltpu.VMEM((d, tn), jnp.bfloat16)],
        compiler_params=pltpu.CompilerParams(
            dimension_semantics=("arbitrary", "arbitrary", "arbitrary"),
            vmem_limit_bytes=VMEM_LIMIT),
        name="proj_mlstm",
    )(h, wt, b_rest)


def _mlstm_chunk(qh, kh, vh, og, zg, b, a, cm, a_row, hw, c_scr, n_scr, m_scr):
    L = qh.shape[0]
    m_state = m_scr[...]
    big_m = jnp.maximum(m_state, cm)
    w_inter = jnp.exp(m_state - big_m) * (QK_DIM ** -0.5)
    clamp = jnp.exp(-b - big_m)
    m_last = big_m[L - 1:L, :]
    ws = jnp.exp(a - m_last)
    keep = jnp.exp(m_state - m_last)
    m_scr[...] = b[L - 1:L, :] + m_last
    neg_m = math.log(QK_DIM ** -0.5) - big_m

    ri = lax.broadcasted_iota(jnp.int32, (L, L), 0)
    ci = lax.broadcasted_iota(jnp.int32, (L, L), 1)
    c_h = c_scr[...]
    n_h = n_scr[...]

    qk = lax.dot_general(qh, kh, (((1,), (1,)), ((), ())), preferred_element_type=jnp.float32)
    scores = jnp.where(ci <= ri, jnp.exp(a_row + neg_m), 0.0) * qk
    num = jnp.dot(scores.astype(jnp.bfloat16), vh, preferred_element_type=jnp.float32)
    num = num + w_inter * jnp.dot(qh, c_h.astype(jnp.bfloat16), preferred_element_type=jnp.float32)
    den = jnp.sum(scores, axis=-1, keepdims=True) \
        + w_inter * jnp.sum(qh.astype(jnp.float32) * n_h, axis=-1, keepdims=True)
    h_out = num * (1.0 / jnp.maximum(jnp.abs(den), clamp))

    vw = (ws * vh.astype(jnp.float32)).astype(jnp.bfloat16)
    kv = lax.dot_general(kh, vw, (((0,), (0,)), ((), ())), preferred_element_type=jnp.float32)
    c_scr[...] = keep * c_h + kv
    n_scr[...] = keep * n_h + jnp.sum(ws * kh.astype(jnp.float32), axis=0, keepdims=True)

    hg = _sigmoid(og.astype(jnp.float32)) * h_out
    ms = jnp.mean(hg * hg, axis=-1, keepdims=True)
    return hg * lax.rsqrt(ms + EPS) * hw * _silu(zg.astype(jnp.float32))


N_GROUPS = 6


def _proj_b_kernel(h_ref, wt_hbm, b_ref, cw_ref,
                   q_ref, k_ref, v_ref, o_ref, z_ref, gcols_ref, gat_ref, hw_ref,
                   ha_ref, hb_ref, sga_ref, sgb_ref,
                   w_scr, stage, sem, c_scr, n_scr, m_scr, carry_scr, *, rest_off):
    head = pl.program_id(0)
    cb = CONV_CB
    d = stage.shape[1]
    steps_per_head = pl.num_programs(1) * pl.num_programs(2)
    t = pl.program_id(1) * pl.num_programs(2) + pl.program_id(2)
    slot = head % 2

    def window_copy(hd, g):
        row0 = pl.multiple_of(rest_off + g * d + hd * cb, SUBLANES)
        return pltpu.make_async_copy(wt_hbm.at[pl.ds(row0, cb), :], stage, sem)

    def store_window(dst_slot, g):
        step = 256
        for r in range(0, cb, step):
            w_scr[dst_slot, :, g * cb + r:g * cb + r + step] = stage[r:r + step, :].T.astype(w_scr.dtype)

    @pl.when((head == 0) & (t == 0))
    def _():
        for g in range(N_GROUPS):
            cp = window_copy(0, g)
            cp.start()
            cp.wait()
            store_window(0, g)

    g_next = t - (steps_per_head - N_GROUPS)
    prefetch = (g_next >= 0) & (head + 1 < pl.num_programs(0))

    @pl.when(prefetch)
    def _():
        window_copy(head + 1, g_next).start()

    @pl.when(pl.program_id(2) == 0)
    def _():
        c_scr[...] = jnp.zeros_like(c_scr)
        n_scr[...] = jnp.zeros_like(n_scr)
        m_scr[...] = jnp.zeros_like(m_scr)
        carry_scr[...] = jnp.zeros_like(carry_scr)

    L = MLSTM_L
    lane = lax.broadcasted_iota(jnp.int32, (L, LANES), 1)
    row = lax.broadcasted_iota(jnp.int32, (L, cb), 0)

    def head_col(rows, group):
        return jnp.sum(jnp.where(lane == head + group * HEADS, gcols_ref[0, rows, :], 0.0),
                       axis=-1, keepdims=True)

    def project(rows, half):
        cols = slice(half * 3 * cb, (half + 1) * 3 * cb)
        acc = jnp.dot(h_ref[0, rows, :], w_scr[slot, :, cols], preferred_element_type=jnp.float32)
        return acc + b_ref[0, :, cols]

    for cc in range(h_ref.shape[1] // L):
        rows = slice(cc * L, (cc + 1) * L)
        p1 = project(rows, 0)
        cu = p1[:, 2 * cb:3 * cb] * p1[:, 0:cb]
        prev = carry_scr[...]
        cu1 = jnp.where(row >= 1, pltpu.roll(cu, 1, 0), prev[7:8, :])
        cu2 = jnp.where(row >= 2, pltpu.roll(cu, 2, 0),
                        jnp.where(row == 1, prev[7:8, :], prev[6:7, :]))
        carry_scr[...] = cu[L - 8:, :]
        conv = cw_ref[0:1, :] * cu2 + cw_ref[1:2, :] * cu1 + cw_ref[2:3, :] * cu
        bconv = p1[:, cb:2 * cb] * conv

        out = _mlstm_chunk(q_ref[0, rows, :], k_ref[0, rows, :], v_ref[0, rows, :],
                           o_ref[0, rows, :], z_ref[0, rows, :],
                           head_col(rows, 0), head_col(rows, 1), head_col(rows, 2),
                           gat_ref[0, pl.ds(head, 1), rows], hw_ref[...], c_scr, n_scr, m_scr)
        ha_ref[0, rows, :] = out.astype(ha_ref.dtype)

        p2 = project(rows, 1)
        hb_ref[0, rows, :] = (bconv * _silu(p2[:, 0:cb])).astype(hb_ref.dtype)
        sga_ref[0, rows, :] = _sigmoid(p2[:, cb:2 * cb]).astype(sga_ref.dtype)
        sgb_ref[0, rows, :] = _sigmoid(p2[:, 2 * cb:3 * cb]).astype(sgb_ref.dtype)

    @pl.when(prefetch)
    def _():
        window_copy(head + 1, g_next).wait()

    for g in range(N_GROUPS):
        @pl.when(prefetch & (g_next == g))
        def _(g=g):
            store_window(1 - slot, g)


def _proj_b_call(h, wt, b_perm, conv_w, q_arr, proj_rest, g_cols, g_at, headnorm_w):
    bsz, s, d = h.shape
    tm = PROJ_TM
    cb = CONV_CB
    rest_off = MLSTM_WIDTH + 2 * HEADS
    assert rest_off % SUBLANES == 0 and cb % SUBLANES == 0 and d % SUBLANES == 0
    assert bsz * (s // tm) >= N_GROUPS

    qk_blk = lambda off: pl.BlockSpec((1, tm, QK_DIM), lambda hd, b, m: (b, m, off + hd))
    v_blk = lambda off: pl.BlockSpec((1, tm, V_DIM), lambda hd, b, m: (b, m, off + hd))
    out_blk = pl.BlockSpec((1, tm, cb), lambda hd, b, m: (b, m, hd))
    out_sds = jax.ShapeDtypeStruct((bsz, s, d), jnp.bfloat16)
    return pl.pallas_call(
        functools.partial(_proj_b_kernel, rest_off=rest_off),
        grid=(HEADS, bsz, s // tm),
        in_specs=[
            pl.BlockSpec((1, tm, d), lambda hd, b, m: (b, m, 0)),
            pl.BlockSpec(memory_space=pl.ANY),
            pl.BlockSpec((1, 1, N_GROUPS * cb), lambda hd, b, m: (hd, 0, 0)),
            pl.BlockSpec((CONV_K, cb), lambda hd, b, m: (0, hd)),
            qk_blk(0), qk_blk(HEADS),
            v_blk(0), v_blk(HEADS), v_blk(2 * HEADS),
            pl.BlockSpec((1, tm, LANES), lambda hd, b, m: (b, m, 0)),
            pl.BlockSpec((1, HEADS, tm), lambda hd, b, m: (b, 0, m)),
            pl.BlockSpec((1, V_DIM), lambda hd, b, m: (0, hd)),
        ],
        out_specs=[out_blk, out_blk, out_blk, out_blk],
        out_shape=[out_sds, out_sds, out_sds, out_sds],
        scratch_shapes=[
            pltpu.VMEM((2, d, N_GROUPS * cb), jnp.bfloat16),
            pltpu.VMEM((cb, d), jnp.float32),
            pltpu.SemaphoreType.DMA(()),
            pltpu.VMEM((QK_DIM, V_DIM), jnp.float32),
            pltpu.VMEM((1, QK_DIM), jnp.float32),
            pltpu.VMEM((1, 1), jnp.float32),
            pltpu.VMEM((8, cb), jnp.float32),
        ],
        compiler_params=pltpu.CompilerParams(
            dimension_semantics=("arbitrary", "arbitrary", "arbitrary"),
            vmem_limit_bytes=VMEM_LIMIT),
        name="proj_conv_mlstm",
    )(h, wt, b_perm, conv_w,
      q_arr, q_arr, proj_rest, proj_rest, proj_rest, g_cols, g_at, headnorm_w)


def _merge_kernel(ha_ref, hb_ref, sga_ref, sgb_ref, x_ref, gate_ref,
                  wa_ref, wb_ref, wo_ref, nf_ref, out_ref, *, final_norm):
    y_a = jnp.dot(ha_ref[0], wa_ref[...], preferred_element_type=jnp.float32)
    y_b = jnp.dot(hb_ref[0], wb_ref[...], preferred_element_type=jnp.float32)
    merged = sga_ref[0].astype(jnp.float32) * y_a + sgb_ref[0].astype(jnp.float32) * y_b
    o = jnp.dot(merged.astype(jnp.bfloat16), wo_ref[...], preferred_element_type=jnp.float32)
    xo = x_ref[0] + gate_ref[0] * o
    if final_norm:
        ms = jnp.mean(xo * xo, axis=-1, keepdims=True)
        xo = xo * lax.rsqrt(ms + EPS) * nf_ref[...]
    out_ref[0] = xo


def _merge_call(h_a, h_b, sg_a, sg_b, x, gate, w_proj_a, w_proj_b, w_out, normf_w, final_norm):
    bsz, s, d = x.shape
    tm = MERGE_TM
    tile = pl.BlockSpec((1, tm, d), lambda b, t: (b, t, 0))
    const = lambda shape: pl.BlockSpec(shape, lambda b, t: (0,) * len(shape),
                                       pipeline_mode=pl.Buffered(1))
    return pl.pallas_call(
        functools.partial(_merge_kernel, final_norm=final_norm),
        grid=(bsz, s // tm),
        in_specs=[
            tile, tile, tile, tile,
            tile,
            pl.BlockSpec((1, 1, d), lambda b, t: (b, 0, 0)),
            const((V_WIDTH, d)), const((d, d)), const((d, d)),
            const((1, d)),
        ],
        out_specs=tile,
        out_shape=jax.ShapeDtypeStruct((bsz, s, d), jnp.float32),
        compiler_params=pltpu.CompilerParams(
            dimension_semantics=("arbitrary", "arbitrary"), vmem_limit_bytes=VMEM_LIMIT),
        name="merge_out",
    )(h_a, h_b, sg_a, sg_b, x, gate, w_proj_a, w_proj_b, w_out, normf_w)


def kernel(x, c, norm1_w, w_ada, b_ada, w_in, b_in, conv_w, headnorm_w,
           w_proj_a, w_proj_b, w_out, normf_w):
    bsz, s, d = x.shape
    depth = norm1_w.shape[0]
    gate_off = MLSTM_WIDTH
    rest_off = gate_off + 2 * HEADS

    c_pad = jnp.pad(c, ((0, 8 - bsz), (0, 0)))
    for l in range(depth):
        mod = _ada_call(c_pad, w_ada[l], b_ada[l][None, :])[:bsz]
        shift, scale, gate = (m[:, None, :] for m in jnp.split(mod, 3, axis=-1))

        wt, b_l = jnp.swapaxes(w_in[l], 0, 1), b_in[l]
        gate_pad = LANES - 2 * HEADS
        assert w_proj_a[l].shape == w_proj_b[l].shape == w_out[l].shape == (d, d)
        h, g_cols, g_at, q_arr, wa16, wb16, wo16 = _norm_q_call(
            x, scale, shift, norm1_w[l][None, :],
            jnp.pad(wt[gate_off:rest_off], ((0, gate_pad), (0, 0))),
            jnp.pad(b_l[None, gate_off:rest_off], ((0, 0), (0, gate_pad))),
            wt, b_l[None, :2 * QK_WIDTH], w_proj_a[l], w_proj_b[l], w_out[l])
        proj_rest = _proj_a_call(h, wt, b_l[None, 2 * QK_WIDTH:gate_off])
        b_perm = b_l[rest_off:].reshape(6, d // CONV_CB, CONV_CB).transpose(1, 0, 2)
        b_perm = b_perm.reshape(d // CONV_CB, 1, 6 * CONV_CB)
        h_a, h_b, sg_a, sg_b = _proj_b_call(h, wt, b_perm, conv_w[l], q_arr, proj_rest, g_cols, g_at,
                                            headnorm_w[l][None, :])
        x = _merge_call(h_a, h_b, sg_a, sg_b, x, gate, wa16, wb16, wo16, normf_w[None, :],
                        final_norm=(l == depth - 1))
    return x
```
